```python
import math
import jax
import jax.numpy as jnp
from jax import lax
import numpy as np

D_MODEL = 1024
BATCH = 2
SEQ = 8192
DEPTH = 2

HEAD_DIM = 64
GROUP_WIDTH = D_MODEL // 4
MIX_WIDTH = 4 * GROUP_WIDTH
ML_HEADS = GROUP_WIDTH // HEAD_DIM
ML_CHUNK = 64
POOL_WINDOWS = (2, 4, 8, 16)
POOL_GROUPS = len(POOL_WINDOWS)
POOL_CH = GROUP_WIDTH // POOL_GROUPS
DIFF_HEADS = GROUP_WIDTH // HEAD_DIM
DIFF_QK_DIM = HEAD_DIM // 2
GQA_Q_HEADS = GROUP_WIDTH // HEAD_DIM
GQA_KV_HEADS = GQA_Q_HEADS // 2
GQA_GROUP = GQA_Q_HEADS // GQA_KV_HEADS
GRID_W = 64
ROPE_THETA = 10000.0
Q_BLOCK = 128
MEM_LEN = 256
CROSS_HEADS = 4
CROSS_HEAD_DIM = D_MODEL // CROSS_HEADS
D_FF = 256 * math.ceil(8 * D_MODEL / 3 / 256)
CONV_W = 3
EPS = 1e-6
IN_SIZES = (GROUP_WIDTH, GROUP_WIDTH, GROUP_WIDTH, GROUP_WIDTH, 4 * ML_HEADS,
            GROUP_WIDTH,
            DIFF_HEADS * 2 * DIFF_QK_DIM, DIFF_HEADS * 2 * DIFF_QK_DIM, DIFF_HEADS * HEAD_DIM,
            GQA_Q_HEADS * HEAD_DIM, GQA_KV_HEADS * HEAD_DIM, GQA_KV_HEADS * HEAD_DIM)
D_IN = sum(IN_SIZES)

kernel_name = 'hybrid_parallel_head_encoder'


def rms_norm(x, g):
    x32 = x.astype(jnp.float32)
    y = x32 * lax.rsqrt(jnp.mean(x32 * x32, axis=-1, keepdims=True) + EPS)
    return (y * g.astype(jnp.float32)).astype(x.dtype)


def rope_cos_sin(pos, dim):
    inv = ROPE_THETA ** (-jnp.arange(0, dim, 2, dtype=jnp.float32) / dim)
    ang = pos.astype(jnp.float32)[:, None] * inv[None, :]
    return jnp.cos(ang), jnp.sin(ang)


def apply_rope(x, cos, sin):
    d2 = x.shape[-1] // 2
    bshape = (cos.shape[0],) + (1,) * (x.ndim - 3) + (cos.shape[1],)
    c = cos.reshape(bshape).astype(x.dtype)
    s = sin.reshape(bshape).astype(x.dtype)
    x1, x2 = x[..., :d2], x[..., d2:]
    return jnp.concatenate([x1 * c - x2 * s, x2 * c + x1 * s], axis=-1)


def sweep_query_blocks(fn, q):
    B, S = q.shape[:2]
    nb = S // Q_BLOCK
    qb = jnp.moveaxis(q.reshape((B, nb, Q_BLOCK) + q.shape[2:]), 1, 0)
    out = lax.map(fn, qb)
    return jnp.moveaxis(out, 0, 1).reshape((B, S) + out.shape[3:])


def mlstm_chunkwise(q, k, v, li, lf):
    B, H, S, d = q.shape
    L = ML_CHUNK
    nc = S // L
    q = q.reshape(B, H, nc, L, d)
    k = k.reshape(B, H, nc, L, d)
    v = v.reshape(B, H, nc, L, d)
    li = li.reshape(B, H, nc, L)
    b = jnp.cumsum(lf.reshape(B, H, nc, L), axis=-1)
    g = b[..., -1]
    a = g[..., None] - b + li
    m_loc = jnp.max(a, axis=-1)
    w = jnp.exp(a - m_loc[..., None])
    C_loc = jnp.einsum('bhcl,bhcld,bhcle->bhcde', w, k, v)
    n_loc = jnp.einsum('bhcl,bhcld->bhcd', w, k)

    def step(carry, inp):
        C, n, m = carry
        g_c, m_l, C_l, n_l = inp
        m_new = jnp.maximum(g_c + m, m_l)
        f_old = jnp.exp(g_c + m - m_new)
        f_loc = jnp.exp(m_l - m_new)
        C_new = f_old[..., None, None] * C + f_loc[..., None, None] * C_l
        n_new = f_old[..., None] * n + f_loc[..., None] * n_l
        return (C_new, n_new, m_new), (C, n, m)

    init = (jnp.zeros((B, H, d, d), jnp.float32), jnp.zeros((B, H, d), jnp.float32),
            jnp.zeros((B, H), jnp.float32))
    xs = (jnp.moveaxis(g, 2, 0), jnp.moveaxis(m_loc, 2, 0),
          jnp.moveaxis(C_loc, 2, 0), jnp.moveaxis(n_loc, 2, 0))
    _, (C_prev, n_prev, m_prev) = lax.scan(step, init, xs)
    C_prev = jnp.moveaxis(C_prev, 0, 2)
    n_prev = jnp.moveaxis(n_prev, 0, 2)
    m_prev = jnp.moveaxis(m_prev, 0, 2)

    lower = jnp.tril(jnp.ones((L, L), dtype=bool))
    dlog = jnp.where(lower, b[..., :, None] - b[..., None, :] + li[..., None, :], -jnp.inf)
    e = b + m_prev[..., None]
    m_t = jnp.maximum(e, jnp.max(dlog, axis=-1))
    wts = jnp.exp(dlog - m_t[..., None]) * jnp.einsum('bhctd,bhcsd->bhcts', q, k)
    s_inter = jnp.exp(e - m_t)
    num = (jnp.einsum('bhcts,bhcsd->bhctd', wts, v)
           + s_inter[..., None] * jnp.einsum('bhctd,bhcde->bhcte', q, C_prev))
    den = jnp.sum(wts, axis=-1) + s_inter * jnp.einsum('bhctd,bhcd->bhct', q, n_prev)
    h = num / jnp.maximum(jnp.abs(den), jnp.exp(-m_t))[..., None]
    return h.reshape(B, H, S, d)


def mlstm_mixer(q, k, v, o, gates, b_i, b_f, g_norm):
    B, S, _ = q.shape
    f32 = jnp.float32
    heads = lambda z: z.astype(f32).reshape(B, S, ML_HEADS, HEAD_DIM).transpose(0, 2, 1, 3)
    qh, kh, vh = heads(q), heads(k) * (HEAD_DIM ** -0.5), heads(v)
    gt = gates.astype(f32).reshape(B, S, 4, ML_HEADS)
    bi = b_i.astype(f32)
    bf = b_f.astype(f32)
    li_f = (gt[:, :, 0] + bi[0]).transpose(0, 2, 1)
    lf_f = jax.nn.log_sigmoid(gt[:, :, 1] + bf[0]).transpose(0, 2, 1)
    li_b = (gt[:, :, 2] + bi[1]).transpose(0, 2, 1)
    lf_b = jax.nn.log_sigmoid(gt[:, :, 3] + bf[1]).transpose(0, 2, 1)
    h_f = mlstm_chunkwise(qh, kh, vh, li_f, lf_f)
    flip = lambda z: jnp.flip(z, axis=2)
    h_b = flip(mlstm_chunkwise(flip(qh), flip(kh), flip(vh), flip(li_b), flip(lf_b)))
    h = (h_f + h_b).transpose(0, 2, 1, 3)
    h = rms_norm(h, g_norm.reshape(ML_HEADS, HEAD_DIM))
    y = h.reshape(B, S, GROUP_WIDTH) * jax.nn.sigmoid(o.astype(f32))
    return y.astype(o.dtype)


def pool_mixer(u, w, scale):
    B, S, _ = u.shape
    ug = u.astype(jnp.float32).reshape(B, S, POOL_GROUPS, POOL_CH)
    cs = jnp.concatenate([jnp.zeros((B, 1, POOL_GROUPS, POOL_CH), jnp.float32),
                          jnp.cumsum(ug, axis=1)], axis=1)
    t = jnp.arange(S)
    outs = []
    for gi, win in enumerate(POOL_WINDOWS):
        lo = jnp.clip(t - win // 2, 0, S - 1)
        hi = jnp.clip(t - win // 2 + win - 1, 0, S - 1)
        cs_g = cs[:, :, gi]
        total = cs_g[:, hi + 1] - cs_g[:, lo]
        cnt = (hi - lo + 1).astype(jnp.float32)[None, :, None]
        outs.append(total / cnt - ug[:, :, gi])
    pooled = jnp.stack(outs, axis=2)
    y = jnp.einsum('bsgc,gce->bsge', pooled, w.astype(jnp.float32))
    return (y.reshape(B, S, GROUP_WIDTH) * scale.astype(jnp.float32)).astype(u.dtype)


def diff_attention(q, k, v, g_q, g_k, lam_params, g_sub, cos, sin, lam_init):
    B, S, _ = q.shape
    q = apply_rope(rms_norm(q.reshape(B, S, DIFF_HEADS, 2, DIFF_QK_DIM), g_q), cos, sin)
    k = apply_rope(rms_norm(k.reshape(B, S, DIFF_HEADS, 2, DIFF_QK_DIM), g_k), cos, sin)
    v = v.reshape(B, S, DIFF_HEADS, HEAD_DIM)
    lp = lam_params.astype(jnp.float32)
    lam = jnp.exp(jnp.sum(lp[0] * lp[1])) - jnp.exp(jnp.sum(lp[2] * lp[3])) + lam_init
    scale = DIFF_QK_DIM ** -0.5

    def block(qb):
        s = jnp.einsum('bqhcd,bshcd->bhcqs', qb, k).astype(jnp.float32) * scale
        p = jax.nn.softmax(s, axis=-1)
        pd = p[:, :, 0] - lam * p[:, :, 1]
        return jnp.einsum('bhqs,bshd->bqhd', pd.astype(v.dtype), v)

    o = sweep_query_blocks(block, q)
    o = rms_norm(o, g_sub) * (1.0 - lam_init)
    return o.reshape(B, S, GROUP_WIDTH)


def gqa_axial(q, k, v, g_q, g_k, row_cs, col_cs):
    B, S, _ = q.shape
    half = HEAD_DIM // 2
    def axial(z):
        return jnp.concatenate([apply_rope(z[..., :half], *row_cs),
                                apply_rope(z[..., half:], *col_cs)], axis=-1)
    q = axial(rms_norm(q.reshape(B, S, GQA_Q_HEADS, HEAD_DIM), g_q))
    k = axial(rms_norm(k.reshape(B, S, GQA_KV_HEADS, HEAD_DIM), g_k))
    v = v.reshape(B, S, GQA_KV_HEADS, HEAD_DIM)
    q = q.reshape(B, S, GQA_KV_HEADS, GQA_GROUP, HEAD_DIM)
    scale = HEAD_DIM ** -0.5

    def block(qb):
        s = jnp.einsum('bqkgd,bskd->bkgqs', qb, k).astype(jnp.float32) * scale
        p = jax.nn.softmax(s, axis=-1)
        return jnp.einsum('bkgqs,bskd->bqkgd', p.astype(v.dtype), v)

    o = sweep_query_blocks(block, q)
    return o.reshape(B, S, GROUP_WIDTH)


def cross_attention(xn, memn, w_q, w_kv, w_o, g_q, g_k):
    B, S, _ = xn.shape
    M = memn.shape[1]
    q = rms_norm((xn @ w_q).reshape(B, S, CROSS_HEADS, CROSS_HEAD_DIM), g_q)
    kv = (memn @ w_kv).reshape(B, M, 2, CROSS_HEADS, CROSS_HEAD_DIM)
    k = rms_norm(kv[:, :, 0], g_k)
    v = kv[:, :, 1]
    s = jnp.einsum('bshd,bmhd->bhsm', q, k).astype(jnp.float32) * (CROSS_HEAD_DIM ** -0.5)
    p = jax.nn.softmax(s, axis=-1)
    o = jnp.einsum('bhsm,bmhd->bshd', p.astype(v.dtype), v).reshape(B, S, D_MODEL)
    return o @ w_o


def conv_ffn(xn, w_in, conv_w, conv_b, w_out):
    h = xn @ w_in
    gate, up = h[..., :D_FF], h[..., D_FF:]
    gp = jnp.pad(gate, ((0, 0), (1, 1), (0, 0)))
    gconv = gp[:, :-2] * conv_w[0] + gp[:, 1:-1] * conv_w[1] + gp[:, 2:] * conv_w[2] + conv_b
    return (jax.nn.silu(gconv) * up) @ w_out


def setup_inputs(seed: int = 0) -> dict:
    key = jax.random.key(seed)
    keys = iter(jax.random.split(key, 40))
    f32 = jnp.float32
    def nrm(shape, scale):
        return scale * jax.random.normal(next(keys), shape, f32)
    def gain(shape):
        return 1.0 + 0.1 * jax.random.normal(next(keys), shape, f32)
    D = D_MODEL
    return {
        'x': jax.random.normal(next(keys), (BATCH, SEQ, D), f32),
        'mem': jax.random.normal(next(keys), (BATCH, MEM_LEN, D), f32),
        'norm_mix': gain((DEPTH, D)),
        'w_in': nrm((DEPTH, D, D_IN), D ** -0.5),
        'ml_bias_i': nrm((DEPTH, 2, ML_HEADS), 0.1),
        'ml_bias_f': 3.0 + 3.0 * jax.random.uniform(next(keys), (DEPTH, 2, ML_HEADS), f32),
        'ml_norm': gain((DEPTH, GROUP_WIDTH)),
        'pool_w': nrm((DEPTH, POOL_GROUPS, POOL_CH, POOL_CH), POOL_CH ** -0.5),
        'pool_scale': gain((DEPTH, GROUP_WIDTH)),
        'diff_qnorm': gain((DEPTH, DIFF_QK_DIM)),
        'diff_knorm': gain((DEPTH, DIFF_QK_DIM)),
        'diff_lambda': nrm((DEPTH, 4, DIFF_QK_DIM), 0.1),
        'diff_subnorm': gain((DEPTH, HEAD_DIM)),
        'gqa_qnorm': gain((DEPTH, HEAD_DIM)),
        'gqa_knorm': gain((DEPTH, HEAD_DIM)),
        'w_out': nrm((DEPTH, MIX_WIDTH, D), MIX_WIDTH ** -0.5),
        'norm_cross': gain((DEPTH, D)),
        'norm_mem': gain((DEPTH, D)),
        'w_cq': nrm((DEPTH, D, D), D ** -0.5),
        'w_ckv': nrm((DEPTH, D, 2 * D), D ** -0.5),
        'cross_qnorm': gain((DEPTH, CROSS_HEAD_DIM)),
        'cross_knorm': gain((DEPTH, CROSS_HEAD_DIM)),
        'w_co': nrm((DEPTH, D, D), D ** -0.5),
        'norm_ffn': gain((DEPTH, D)),
        'w_ffn_in': nrm((DEPTH, D, 2 * D_FF), D ** -0.5),
        'ffn_conv': nrm((DEPTH, CONV_W, D_FF), CONV_W ** -0.5),
        'ffn_conv_b': nrm((DEPTH, D_FF), 0.02),
        'w_ffn_out': nrm((DEPTH, D_FF, D), D_FF ** -0.5),
    }


def reference(x, mem, norm_mix, w_in, ml_bias_i, ml_bias_f, ml_norm, pool_w, pool_scale,
              diff_qnorm, diff_knorm, diff_lambda, diff_subnorm, gqa_qnorm, gqa_knorm, w_out,
              norm_cross, norm_mem, w_cq, w_ckv, cross_qnorm, cross_knorm, w_co,
              norm_ffn, w_ffn_in, ffn_conv, ffn_conv_b, w_ffn_out):
    B, S, _ = x.shape
    rows = S // GRID_W
    pos = jnp.arange(S)
    row = jnp.repeat(jnp.arange(rows), GRID_W)
    col = pos - row * GRID_W
    cos1, sin1 = rope_cos_sin(pos, DIFF_QK_DIM)
    row_cs = rope_cos_sin(row, HEAD_DIM // 2)
    col_cs = rope_cos_sin(col, HEAD_DIM // 2)
    splits, acc = [], 0
    for sz in IN_SIZES[:-1]:
        acc += sz
        splits.append(acc)

    for l in range(DEPTH):
        lam_init = 0.8 - 0.6 * math.exp(-0.3 * l)
        h = rms_norm(x, norm_mix[l])
        (ml_q, ml_k, ml_v, ml_o, ml_g, pool_u,
         d_q, d_k, d_v, g_q, g_k, g_v) = jnp.split(h @ w_in[l], splits, axis=-1)
        y_a = mlstm_mixer(ml_q, ml_k, ml_v, ml_o, ml_g, ml_bias_i[l], ml_bias_f[l], ml_norm[l])
        y_b = pool_mixer(pool_u, pool_w[l], pool_scale[l])
        y_c = diff_attention(d_q, d_k, d_v, diff_qnorm[l], diff_knorm[l], diff_lambda[l],
                             diff_subnorm[l], cos1, sin1, lam_init)
        y_d = gqa_axial(g_q, g_k, g_v, gqa_qnorm[l], gqa_knorm[l], row_cs, col_cs)
        x = x + jnp.concatenate([y_a, y_b, y_c, y_d], axis=-1) @ w_out[l]
        x = x + cross_attention(rms_norm(x, norm_cross[l]), rms_norm(mem, norm_mem[l]),
                                w_cq[l], w_ckv[l], w_co[l], cross_qnorm[l], cross_knorm[l])
        x = x + conv_ffn(rms_norm(x, norm_ffn[l]), w_ffn_in[l], ffn_conv[l], ffn_conv_b[l],
                         w_ffn_out[l])
    return x
```

```python
import functools
import math

import jax
import jax.numpy as jnp
from jax import lax
from jax.experimental import pallas as pl
from jax.experimental.pallas import tpu as pltpu

F32 = jnp.float32
BF16 = jnp.bfloat16

D_MODEL = 1024
HEAD_DIM = 64
GROUP_WIDTH = 256
N_HEADS = 4
POOL_WINDOWS = (2, 4, 8, 16)
DIFF_QK_DIM = 32
GQA_KV_HEADS = 2
GRID_W = 64
ROPE_THETA = 10000.0
CROSS_HEADS = 4
CROSS_HEAD_DIM = 256
D_FF = 2816
EPS = 1e-6

LANES = 128
SUBLANES = 8
VMEM_LIMIT_BYTES = 56 * 1024 * 1024

V_AUG = HEAD_DIM + 16
ML_CHUNK = 128
NEG_BIG = -1e30

_NT = (((1,), (1,)), ((), ()))


def _dot(a, b):
    return jnp.dot(a, b, preferred_element_type=F32)


def _dot_nt(a, b):
    return lax.dot_general(a, b, _NT, preferred_element_type=F32)


def _dot_f32(a, b):
    return jnp.dot(a, b, preferred_element_type=F32, precision=lax.Precision.HIGHEST)


def _params(n_axes):
    return pltpu.CompilerParams(dimension_semantics=("arbitrary",) * n_axes,
                                vmem_limit_bytes=VMEM_LIMIT_BYTES)


def _const_spec(shape):
    nd = len(shape)
    return pl.BlockSpec(shape, lambda *_: (0,) * nd, pipeline_mode=pl.Buffered(1))


def _rms(x, g):
    ms = jnp.mean(x * x, axis=-1, keepdims=True)
    return x * lax.rsqrt(ms + EPS) * g


RM_Q, RM_K, RM_V, RM_O, RM_POOL = 0, 512, 1024, 1536, 2048
RM_COLS = 2304
T_DQ, T_DK, T_GQ, T_GK, T_DV, T_GV, T_MK, T_GATE = 0, 256, 512, 768, 896, 1152, 1280, 1792
T_ROWS = 1808


def _norm_rope_t(z, norm_rows, gain, cos64, sin64, scale):
    r, t = z.shape
    zg = z.reshape(r // norm_rows, norm_rows, t)
    ms = jnp.mean(zg * zg, axis=1, keepdims=True)
    zn = (zg * lax.rsqrt(ms + EPS) * gain[None]).reshape(r // 64, 64, t)
    sw = jnp.concatenate([zn[:, 16:32], zn[:, 0:16], zn[:, 48:64], zn[:, 32:48]], axis=1)
    out = zn * cos64[None] + sw * sin64[None]
    return (out * scale).reshape(r, t)


def _with_ones_rows(vt, heads):
    t = vt.shape[1]
    v3 = vt.reshape(heads, HEAD_DIM, t)
    ones = jnp.ones((heads, V_AUG - HEAD_DIM, t), F32)
    return jnp.concatenate([v3, ones], axis=1).reshape(heads * V_AUG, t)


def _inproj_kernel(x_ref, g_ref, wrm_ref, wg_ref, wt_ref, cscale_ref, cadd_ref, tabs_ref,
                   gqd_ref, gkd_ref, gqg_ref, gkg_ref,
                   rm_ref, gates_ref, gates_t_ref, qtd_ref, kd_ref, vtd_ref,
                   qtg_ref, kg_ref, vtg_ref, mkt_ref):
    x = x_ref[0]
    h = _rms(x, g_ref[...]).astype(BF16)
    zr = _dot(h, wrm_ref[...])
    rm_ref[0] = (zr * cscale_ref[...] + cadd_ref[...]).astype(BF16)
    gates_ref[0] = _dot(h, wg_ref[...])
    zt = _dot_nt(wt_ref[...], h)

    tabs = tabs_ref[...]
    c1, s1, rc, rs, cc, cs = (tabs[i] for i in range(6))
    cos_d = jnp.concatenate([c1, c1, c1, c1], axis=0)
    sin_d = jnp.concatenate([-s1, s1, -s1, s1], axis=0)
    cos_g = jnp.concatenate([rc, rc, cc, cc], axis=0)
    sin_g = jnp.concatenate([-rs, rs, -cs, cs], axis=0)

    qd = _norm_rope_t(zt[T_DQ:T_DQ + 256], DIFF_QK_DIM, gqd_ref[...], cos_d, sin_d, DIFF_QK_DIM ** -0.5)
    qtd_ref[0] = qd.astype(BF16)
    kd = _norm_rope_t(zt[T_DK:T_DK + 256], DIFF_QK_DIM, gkd_ref[...], cos_d, sin_d, 1.0)
    kd_ref[0] = kd.T.astype(BF16)
    qg = _norm_rope_t(zt[T_GQ:T_GQ + 256], HEAD_DIM, gqg_ref[...], cos_g, sin_g, HEAD_DIM ** -0.5)
    qtg_ref[0] = qg.astype(BF16)
    kg = _norm_rope_t(zt[T_GK:T_GK + 128], HEAD_DIM, gkg_ref[...], cos_g, sin_g, 1.0)
    kg_ref[0] = kg.T.astype(BF16)
    vtd_ref[0] = _with_ones_rows(zt[T_DV:T_DV + 256], N_HEADS).astype(BF16)
    vtg_ref[0] = _with_ones_rows(zt[T_GV:T_GV + 128], GQA_KV_HEADS).astype(BF16)
    mkt_ref[0] = (zt[T_MK:T_MK + 512] * (HEAD_DIM ** -0.5)).astype(BF16)
    gates_t_ref[0] = zt[T_GATE:T_GATE + 16]


def _pad_heads_cols(w):
    d = w.shape[0]
    w4 = w.reshape(d, N_HEADS, HEAD_DIM)
    return jnp.concatenate([w4, jnp.zeros_like(w4)], axis=2).reshape(d, N_HEADS * LANES)


def _inproj_weights(w_in):
    sizes = (256, 256, 256, 256, 16, 256, 256, 256, 256, 256, 128, 128)
    offs = [0]
    for s in sizes:
        offs.append(offs[-1] + s)
    (ml_q, ml_k, ml_v, ml_o, ml_g, pool, d_q, d_k, d_v, g_q, g_k, g_v) = (
        w_in[:, offs[i]:offs[i + 1]] for i in range(12))
    w_rm = jnp.concatenate([_pad_heads_cols(ml_q), _pad_heads_cols(ml_k), _pad_heads_cols(ml_v),
                            _pad_heads_cols(ml_o), pool], axis=1).astype(BF16)
    w_gate = jnp.concatenate([ml_g, jnp.zeros((D_MODEL, LANES - 16), F32)], axis=1).astype(BF16)
    w_t = jnp.concatenate([d_q, d_k, g_q, g_k, d_v, g_v, _pad_heads_cols(ml_k), ml_g], axis=1).T.astype(BF16)
    return w_rm, w_gate, w_t


def _inproj_col_consts():
    lane = jnp.arange(RM_COLS)
    in_k = (lane >= RM_K) & (lane < RM_V)
    cscale = jnp.where(in_k, HEAD_DIM ** -0.5, 1.0).astype(F32)[None]
    ones_col = (lane >= RM_V) & (lane < RM_O) & ((lane % LANES) == HEAD_DIM)
    cadd = jnp.where(ones_col, 1.0, 0.0).astype(F32)[None]
    return cscale, cadd


def _in_projection(x, g, w_rm, w_gate, w_t, tabs, gqd, gkd, gqg, gkg, tm=512):
    b, s, d = x.shape
    cscale, cadd = _inproj_col_consts()
    bc = lambda v: jnp.broadcast_to(v.astype(F32)[:, None], (v.shape[0], tm))
    grid = (b, s // tm)
    row_blk = lambda w: pl.BlockSpec((1, tm, w), lambda bi, i: (bi, i, 0))
    col_blk = lambda r: pl.BlockSpec((1, r, tm), lambda bi, i: (bi, 0, i))
    out_shapes = (
        jax.ShapeDtypeStruct((b, s, RM_COLS), BF16),
        jax.ShapeDtypeStruct((b, s, LANES), F32),
        jax.ShapeDtypeStruct((b, 16, s), F32),
        jax.ShapeDtypeStruct((b, 256, s), BF16),
        jax.ShapeDtypeStruct((b, s, 256), BF16),
        jax.ShapeDtypeStruct((b, N_HEADS * V_AUG, s), BF16),
        jax.ShapeDtypeStruct((b, 256, s), BF16),
        jax.ShapeDtypeStruct((b, s, 128), BF16),
        jax.ShapeDtypeStruct((b, GQA_KV_HEADS * V_AUG, s), BF16),
        jax.ShapeDtypeStruct((b, 512, s), BF16),
    )
    out_specs = (row_blk(RM_COLS), row_blk(LANES), col_blk(16), col_blk(256), row_blk(256),
                 col_blk(N_HEADS * V_AUG), col_blk(256), row_blk(128), col_blk(GQA_KV_HEADS * V_AUG),
                 col_blk(512))
    in_specs = [
        row_blk(d),
        _const_spec((1, d)),
        _const_spec(w_rm.shape), _const_spec(w_gate.shape), _const_spec(w_t.shape),
        _const_spec((1, RM_COLS)), _const_spec((1, RM_COLS)),
        pl.BlockSpec((6, 16, tm), lambda bi, i: (0, 0, i)),
        _const_spec((DIFF_QK_DIM, tm)), _const_spec((DIFF_QK_DIM, tm)),
        _const_spec((HEAD_DIM, tm)), _const_spec((HEAD_DIM, tm)),
    ]
    return pl.pallas_call(
        _inproj_kernel, grid=grid, in_specs=in_specs, out_specs=out_specs, out_shape=out_shapes,
        compiler_params=_params(2), name="in_projection",
    )(x, g[None].astype(F32), w_rm, w_gate, w_t, cscale, cadd, tabs, bc(gqd), bc(gkd), bc(gqg), bc(gkg))


def _attn_kernel(*refs, combos, width, tq, tk, n_extra, finalize):
    qt_ref, k_ref, vt_ref = refs[:3]
    extra = refs[3:3 + n_extra]
    o_ref = refs[3 + n_extra]
    qpad_sc, m_sc, acc_sc = refs[4 + n_extra:]
    n_kv = k_ref.shape[1] // tk

    for c, (q_row0, q_rows, k_off, _) in enumerate(combos):
        pieces = []
        if k_off:
            pieces.append(jnp.zeros((k_off, tq), BF16))
        pieces.append(qt_ref[0, q_row0:q_row0 + q_rows, :])
        if width - k_off - q_rows:
            pieces.append(jnp.zeros((width - k_off - q_rows, tq), BF16))
        qpad_sc[c] = jnp.concatenate(pieces, axis=0) if len(pieces) > 1 else pieces[0]
    m_sc[...] = jnp.full(m_sc.shape, NEG_BIG, F32)
    acc_sc[...] = jnp.zeros(acc_sc.shape, F32)

    def body(j, carry):
        start = pl.multiple_of(j * tk, tk)
        k_t = k_ref[0, pl.ds(start, tk), :]
        for c, (_, _, _, v_row0) in enumerate(combos):
            st = _dot(k_t, qpad_sc[c])
            m_prev = m_sc[c]
            m_new = jnp.maximum(m_prev, jnp.max(st, axis=0, keepdims=True))
            p = jnp.exp(st - m_new).astype(BF16)
            alpha = jnp.exp(m_prev - m_new)
            vt_t = vt_ref[0, v_row0:v_row0 + V_AUG, pl.ds(start, tk)]
            acc_sc[c] = alpha * acc_sc[c] + _dot(vt_t, p)
            m_sc[c] = m_new
        return carry

    lax.fori_loop(0, n_kv, body, 0)
    out_t = finalize([acc_sc[c] for c in range(len(combos))], extra)
    o_ref[0] = out_t.T.astype(BF16)


def _finalize_diff(accs, extra, *, lam_init):
    lam_ref, gsub_ref = extra
    lp = lam_ref[...]
    lam = (jnp.exp(jnp.sum(lp[0:1] * lp[1:2], axis=1, keepdims=True))
           - jnp.exp(jnp.sum(lp[2:3] * lp[3:4], axis=1, keepdims=True)) + lam_init)
    outs = []
    for h in range(N_HEADS):
        a1, a2 = accs[2 * h], accs[2 * h + 1]
        o = (a1[:HEAD_DIM] / a1[HEAD_DIM:HEAD_DIM + 1]
             - lam * (a2[:HEAD_DIM] / a2[HEAD_DIM:HEAD_DIM + 1]))
        ms = jnp.mean(o * o, axis=0, keepdims=True)
        outs.append(o * lax.rsqrt(ms + EPS) * gsub_ref[...] * (1.0 - lam_init))
    return jnp.concatenate(outs, axis=0)


def _finalize_gqa(accs, extra):
    return jnp.concatenate([a[:HEAD_DIM] / a[HEAD_DIM:HEAD_DIM + 1] for a in accs], axis=0)


def _attention(qt, k, vt, extra, *, combos, finalize, tq=256, tk=512):
    b, _, s = qt.shape
    width = k.shape[2]
    n_c = len(combos)
    kern = functools.partial(_attn_kernel, combos=combos, width=width, tq=tq, tk=tk,
                             n_extra=len(extra), finalize=finalize)
    in_specs = [
        pl.BlockSpec((1, qt.shape[1], tq), lambda bi, i: (bi, 0, i)),
        pl.BlockSpec((1, s, width), lambda bi, i: (bi, 0, 0)),
        pl.BlockSpec((1, vt.shape[1], s), lambda bi, i: (bi, 0, 0)),
    ] + [_const_spec(e.shape) for e in extra]
    return pl.pallas_call(
        kern, grid=(b, s // tq), in_specs=in_specs,
        out_specs=pl.BlockSpec((1, tq, 256), lambda bi, i: (bi, i, 0)),
        out_shape=jax.ShapeDtypeStruct((b, s, 256), BF16),
        scratch_shapes=[pltpu.VMEM((n_c, width, tq), BF16),
                        pltpu.VMEM((n_c, 1, tq), F32),
                        pltpu.VMEM((n_c, V_AUG, tq), F32)],
        compiler_params=_params(2), name="attention_%d" % width,
    )(qt, k, vt, *extra)


def _diff_attention(qt, k, vt, lam_params, g_sub, lam_init, tq=256):
    combos = tuple((64 * h + 32 * c, 32, 64 * h + 32 * c, V_AUG * h)
                   for h in range(N_HEADS) for c in range(2))
    gsub_b = jnp.broadcast_to(g_sub.astype(F32)[:, None], (HEAD_DIM, tq))
    fin = functools.partial(_finalize_diff, lam_init=lam_init)
    return _attention(qt, k, vt, (lam_params.astype(F32), gsub_b), combos=combos, finalize=fin, tq=tq)


def _gqa_attention(qt, k, vt, tq=256):
    combos = tuple((64 * qh, 64, 64 * (qh // 2), V_AUG * (qh // 2)) for qh in range(N_HEADS))
    return _attention(qt, k, vt, (), combos=combos, finalize=_finalize_gqa, tq=tq)


def _log_sigmoid(x):
    return jnp.minimum(x, 0.0) - jnp.log(1.0 + jnp.exp(-jnp.abs(x)))


def _mlstm_kernel(qf_ref, kf_ref, vf_ref, ktf_ref, gf_ref, gtf_ref,
                  qb_ref, kb_ref, vb_ref, ktb_ref, gb_ref, gtb_ref,
                  brow_ref, bcol_ref, hf_ref, hb_ref, c_sc, m_sc):
    L = ML_CHUNK

    @pl.when(pl.program_id(1) == 0)
    def _():
        c_sc[...] = jnp.zeros(c_sc.shape, F32)
        m_sc[...] = jnp.zeros(m_sc.shape, F32)

    row = lax.broadcasted_iota(jnp.int32, (L, L), 0)
    col = lax.broadcasted_iota(jnp.int32, (L, L), 1)
    lower = col <= row
    upper = col >= row
    lower_f = lower.astype(F32)
    upper_f = upper.astype(F32)
    lane = lax.broadcasted_iota(jnp.int32, (1, LANES), 1)
    forget_lane = (lane % 8) >= 4
    grow = lax.broadcasted_iota(jnp.int32, (16, 1), 0)
    forget_row = (grow % 8) >= 4
    keep = (lax.broadcasted_iota(jnp.int32, (1, LANES), 1) < HEAD_DIM).astype(F32)

    dirs = (
        (0, qf_ref, kf_ref, vf_ref, ktf_ref, gf_ref, gtf_ref, hf_ref, lower, lower_f, upper_f, L - 1),
        (1, qb_ref, kb_ref, vb_ref, ktb_ref, gb_ref, gtb_ref, hb_ref, upper, upper_f, lower_f, 0),
    )
    for d, q_ref, k_ref, v_ref, kt_ref, g_ref, gt_ref, h_ref, mask, tri_c, tri_r, last in dirs:
        g = g_ref[0] + brow_ref[...]
        gp = jnp.where(forget_lane, _log_sigmoid(g), g)
        gt = gt_ref[0] + bcol_ref[...]
        gtp = jnp.where(forget_row, _log_sigmoid(gt), gt)
        bcol_all = _dot_f32(tri_c, gp)
        brow_all = _dot_f32(gtp, tri_r)
        for h in range(N_HEADS):
            ii, fi = d * 8 + h, d * 8 + 4 + h
            idx = d * N_HEADS + h
            li_col = gp[:, ii:ii + 1]
            b_col = bcol_all[:, fi:fi + 1]
            li_row = gtp[ii:ii + 1, :]
            b_row = brow_all[fi:fi + 1, :]
            g_tot = b_row[:, last:last + 1]
            m_prev = m_sc[idx][0:1, 0:1]

            qh = q_ref[0, :, h * LANES:(h + 1) * LANES]
            kh = k_ref[0, :, h * LANES:(h + 1) * LANES]
            vh = v_ref[0, :, h * LANES:(h + 1) * LANES]
            kth = kt_ref[0, h * LANES:(h + 1) * LANES, :]

            dlog = jnp.where(mask, b_col - b_row + li_row, -jnp.inf)
            e_col = b_col + m_prev
            m_t = jnp.maximum(e_col, jnp.max(dlog, axis=1, keepdims=True))
            wts = (jnp.exp(dlog - m_t) * _dot_nt(qh, kh)).astype(BF16)
            s_inter = jnp.exp(e_col - m_t)
            c_prev = c_sc[idx]
            tot = _dot(wts, vh) + s_inter * _dot(qh, c_prev.astype(BF16))
            den = tot[:, HEAD_DIM:HEAD_DIM + 1]
            hout = tot / jnp.maximum(jnp.abs(den), jnp.exp(-m_t))
            h_ref[0, :, h * LANES:(h + 1) * LANES] = (hout * keep).astype(BF16)

            a_row = g_tot - b_row + li_row
            m_loc = jnp.max(a_row, axis=1, keepdims=True)
            w_col = jnp.exp(g_tot - b_col + li_col - m_loc)
            c_loc = _dot(kth, (vh.astype(F32) * w_col).astype(BF16))
            m_new = jnp.maximum(g_tot + m_prev, m_loc)
            f_old = jnp.exp(g_tot + m_prev - m_new)
            f_loc = jnp.exp(m_loc - m_new)
            c_sc[idx] = f_old * c_prev + f_loc * c_loc
            m_sc[idx] = jnp.broadcast_to(m_new, (SUBLANES, LANES))


def _mlstm(rm, mkt, gates, gates_t, b_i, b_f):
    b, s, _ = rm.shape
    L = ML_CHUNK
    nc = s // L
    bias = jnp.stack([b_i[0], b_f[0], b_i[1], b_f[1]]).reshape(16).astype(F32)
    bias_row = jnp.concatenate([bias, jnp.zeros((LANES - 16,), F32)])[None]
    bias_col = jnp.broadcast_to(bias[:, None], (16, L))
    fwd = lambda cblk: (lambda bi, i: (bi, i, cblk))
    bwd = lambda cblk: (lambda bi, i: (bi, nc - 1 - i, cblk))
    in_specs = []
    for mk in (fwd, bwd):
        in_specs += [pl.BlockSpec((1, L, 512), mk(RM_Q // 512)),
                     pl.BlockSpec((1, L, 512), mk(RM_K // 512)),
                     pl.BlockSpec((1, L, 512), mk(RM_V // 512)),
                     pl.BlockSpec((1, 512, L), (lambda bi, i: (bi, 0, i)) if mk is fwd
                                  else (lambda bi, i: (bi, 0, nc - 1 - i))),
                     pl.BlockSpec((1, L, LANES), mk(0)),
                     pl.BlockSpec((1, 16, L), (lambda bi, i: (bi, 0, i)) if mk is fwd
                                  else (lambda bi, i: (bi, 0, nc - 1 - i)))]
    in_specs += [_const_spec((1, LANES)), _const_spec((16, L))]
    out_specs = (pl.BlockSpec((1, L, 512), lambda bi, i: (bi, i, 0)),
                 pl.BlockSpec((1, L, 512), lambda bi, i: (bi, nc - 1 - i, 0)))
    out_shape = (jax.ShapeDtypeStruct((b, s, 512), BF16), jax.ShapeDtypeStruct((b, s, 512), BF16))
    return pl.pallas_call(
        _mlstm_kernel, grid=(b, nc), in_specs=in_specs, out_specs=out_specs, out_shape=out_shape,
        scratch_shapes=[pltpu.VMEM((2 * N_HEADS, LANES, LANES), F32),
                        pltpu.VMEM((2 * N_HEADS, SUBLANES, LANES), F32)],
        compiler_params=_params(2), name="mlstm",
    )(rm, rm, rm, mkt, gates, gates_t, rm, rm, rm, mkt, gates, gates_t, bias_row, bias_col)


def _outproj_kernel(hf_ref, hb_ref, o_ref, pm_ref, pp_ref, pn_ref, yc_ref, yd_ref, x_ref,
                    gml_ref, wpool_ref, pscale_ref, wa_ref, wb_ref, wc_ref, wd_ref, out_ref, *, tm, seq):
    i = pl.program_id(1)
    n_t = pl.num_programs(1)

    hs = hf_ref[0].astype(F32) + hb_ref[0].astype(F32)
    og = o_ref[0].astype(F32)
    parts = []
    for h in range(N_HEADS):
        sl = slice(h * LANES, (h + 1) * LANES)
        hh = hs[:, sl]
        ms = jnp.sum(hh * hh, axis=-1, keepdims=True) * (1.0 / HEAD_DIM)
        gate = 1.0 / (1.0 + jnp.exp(-og[:, sl]))
        parts.append(hh * lax.rsqrt(ms + EPS) * gml_ref[:, sl] * gate)
    ya = jnp.concatenate(parts, axis=1).astype(BF16)

    prev = jnp.where(i > 0, pp_ref[0].astype(F32), 0.0)
    nxt = jnp.where(i < n_t - 1, pn_ref[0].astype(F32), 0.0)
    u = jnp.concatenate([prev, pm_ref[0].astype(F32), nxt], axis=0)
    n = tm + 2 * SUBLANES
    sh = lambda v, k: pltpu.roll(v, k % n, 0)
    w2 = u + sh(u, 1)
    w4 = sh(w2, 1) + sh(w2, -1)
    w8 = sh(w4, 2) + sh(w4, -2)
    w16 = sh(w8, 4) + sh(w8, -4)
    core = slice(SUBLANES, SUBLANES + tm)
    lane = lax.broadcasted_iota(jnp.int32, (1, GROUP_WIDTH), 1)
    t = (i * tm + lax.broadcasted_iota(jnp.int32, (tm, 1), 0)).astype(F32)
    total = jnp.where(lane < 64, w2[core], jnp.where(lane < 128, w4[core],
                      jnp.where(lane < 192, w8[core], w16[core])))
    half = jnp.where(lane < 64, 1.0, jnp.where(lane < 128, 2.0, jnp.where(lane < 192, 4.0, 8.0)))
    lo = jnp.maximum(t - half, 0.0)
    hi = jnp.minimum(t + half - 1.0, seq - 1.0)
    pooled = total / (hi - lo + 1.0) - u[core]
    yb = (_dot(pooled.astype(BF16), wpool_ref[...]) * pscale_ref[...]).astype(BF16)

    out_ref[0] = (x_ref[0] + _dot(ya, wa_ref[...]) + _dot(yb, wb_ref[...])
                  + _dot(yc_ref[0], wc_ref[...]) + _dot(yd_ref[0], wd_ref[...]))


def _out_projection(hf, hb, rm, yc, yd, x, ml_norm, pool_w, pool_scale, w_out, tm=512):
    b, s, d = x.shape
    wa = w_out[0:256].reshape(N_HEADS, HEAD_DIM, d)
    wa = jnp.concatenate([wa, jnp.zeros_like(wa)], axis=1).reshape(N_HEADS * LANES, d).astype(BF16)
    wb, wc, wd = (w_out[256 * j:256 * (j + 1)].astype(BF16) for j in (1, 2, 3))
    gml = ml_norm.reshape(N_HEADS, HEAD_DIM)
    gml = jnp.concatenate([gml, jnp.zeros_like(gml)], axis=1).reshape(1, N_HEADS * LANES).astype(F32)
    wpool = jax.scipy.linalg.block_diag(*[pool_w[g] for g in range(4)]).astype(BF16)
    nb8 = tm // SUBLANES
    last8 = s // SUBLANES - 1
    row = lambda w, cb=0: pl.BlockSpec((1, tm, w), lambda bi, i: (bi, i, cb))
    in_specs = [
        row(512), row(512), row(512, RM_O // 512),
        row(256, RM_POOL // 256),
        pl.BlockSpec((1, SUBLANES, 256), lambda bi, i: (bi, jnp.maximum(i * nb8 - 1, 0), RM_POOL // 256)),
        pl.BlockSpec((1, SUBLANES, 256), lambda bi, i: (bi, jnp.minimum((i + 1) * nb8, last8), RM_POOL // 256)),
        row(256), row(256), row(d),
        _const_spec((1, 512)), _const_spec((256, 256)), _const_spec((1, 256)),
        _const_spec((512, d)), _const_spec((256, d)), _const_spec((256, d)), _const_spec((256, d)),
    ]
    kern = functools.partial(_outproj_kernel, tm=tm, seq=s)
    return pl.pallas_call(
        kern, grid=(b, s // tm), in_specs=in_specs, out_specs=row(d),
        out_shape=jax.ShapeDtypeStruct((b, s, d), F32),
        compiler_params=_params(2), name="out_projection",
    )(hf, hb, rm, rm, rm, rm, yc, yd, x, gml, wpool, pool_scale[None].astype(F32), wa, wb, wc, wd)


def _memkv_kernel(mem_ref, g_ref, w_ref, gk_ref, k_ref, v_ref):
    mn = _rms(mem_ref[0], g_ref[...]).astype(BF16)
    kv = _dot(mn, w_ref[...])
    ks = []
    for h in range(CROSS_HEADS):
        ks.append(_rms(kv[:, h * CROSS_HEAD_DIM:(h + 1) * CROSS_HEAD_DIM], gk_ref[...]))
    k_ref[0] = jnp.concatenate(ks, axis=1).astype(BF16)
    v_ref[0] = kv[:, D_MODEL:].astype(BF16)


def _mem_kv(mem, g_mem, w_kv, g_k):
    b, m, d = mem.shape
    blk = pl.BlockSpec((1, m, d), lambda bi: (bi, 0, 0))
    return pl.pallas_call(
        _memkv_kernel, grid=(b,),
        in_specs=[blk, _const_spec((1, d)), _const_spec((d, 2 * d)), _const_spec((1, CROSS_HEAD_DIM))],
        out_specs=(blk, blk),
        out_shape=(jax.ShapeDtypeStruct((b, m, d), BF16), jax.ShapeDtypeStruct((b, m, d), BF16)),
        compiler_params=_params(1), name="mem_kv",
    )(mem, g_mem[None].astype(F32), w_kv.astype(BF16), g_k[None].astype(F32))


def _cross_kernel(x_ref, g_ref, wq_ref, gq_ref, k_ref, v_ref, wo_ref, out_ref):
    x = x_ref[0]
    xn = _rms(x, g_ref[...]).astype(BF16)
    q = _dot(xn, wq_ref[...])
    outs = []
    for h in range(CROSS_HEADS):
        sl = slice(h * CROSS_HEAD_DIM, (h + 1) * CROSS_HEAD_DIM)
        qh = (_rms(q[:, sl], gq_ref[...]) * (CROSS_HEAD_DIM ** -0.5)).astype(BF16)
        s = _dot_nt(qh, k_ref[0, :, sl])
        p = jnp.exp(s - jnp.max(s, axis=-1, keepdims=True))
        l = jnp.sum(p, axis=-1, keepdims=True)
        outs.append((_dot(p.astype(BF16), v_ref[0, :, sl]) / l).astype(BF16))
    o = jnp.concatenate(outs, axis=1)
    out_ref[0] = x + _dot(o, wo_ref[...])


def _cross_attention(x, g, w_q, g_q, k, v, w_o, tm=512):
    b, s, d = x.shape
    m = k.shape[1]
    row = pl.BlockSpec((1, tm, d), lambda bi, i: (bi, i, 0))
    kvb = pl.BlockSpec((1, m, d), lambda bi, i: (bi, 0, 0))
    return pl.pallas_call(
        _cross_kernel, grid=(b, s // tm),
        in_specs=[row, _const_spec((1, d)), _const_spec((d, d)), _const_spec((1, CROSS_HEAD_DIM)),
                  kvb, kvb, _const_spec((d, d))],
        out_specs=row, out_shape=jax.ShapeDtypeStruct((b, s, d), F32),
        compiler_params=_params(2), name="cross_attention",
    )(x, g[None].astype(F32), w_q.astype(BF16), g_q[None].astype(F32), k, v, w_o.astype(BF16))


FF_CHUNK = 256


def _ffn_kernel(xm_ref, xp_ref, xn_ref, g_ref, win_ref, cw_ref, cb_ref, wout_ref, out_ref, *, tm):
    i = pl.program_id(1)
    n_t = pl.num_programs(1)
    xm = xm_ref[0]
    xe = jnp.concatenate([xp_ref[0], xm, xn_ref[0]], axis=0)
    hn = _rms(xe, g_ref[...]).astype(BF16)
    n = tm + 2 * SUBLANES
    core = slice(SUBLANES, SUBLANES + tm)
    hc = hn[core]
    r = lax.broadcasted_iota(jnp.int32, (n, 1), 0)
    live = jnp.logical_and(jnp.logical_or(i > 0, r >= SUBLANES),
                           jnp.logical_or(i < n_t - 1, r < SUBLANES + tm)).astype(F32)
    acc = xm
    for c in range(D_FF // FF_CHUNK):
        gs = slice(c * FF_CHUNK, (c + 1) * FF_CHUNK)
        us = slice(D_FF + c * FF_CHUNK, D_FF + (c + 1) * FF_CHUNK)
        gate = _dot(hn, win_ref[:, gs]) * live
        cw = cw_ref[:, gs]
        gconv = (pltpu.roll(gate, 1, 0)[core] * cw[0:1] + gate[core] * cw[1:2]
                 + pltpu.roll(gate, n - 1, 0)[core] * cw[2:3] + cb_ref[:, gs])
        up = _dot(hc, win_ref[:, us])
        act = (gconv / (1.0 + jnp.exp(-gconv)) * up).astype(BF16)
        acc = acc + _dot(act, wout_ref[gs, :])
    out_ref[0] = acc


def _conv_ffn(x, g, w_in, conv_w, conv_b, w_out, tm=256):
    b, s, d = x.shape
    nb8 = tm // SUBLANES
    last8 = s // SUBLANES - 1
    row = pl.BlockSpec((1, tm, d), lambda bi, i: (bi, i, 0))
    in_specs = [
        row,
        pl.BlockSpec((1, SUBLANES, d), lambda bi, i: (bi, jnp.maximum(i * nb8 - 1, 0), 0)),
        pl.BlockSpec((1, SUBLANES, d), lambda bi, i: (bi, jnp.minimum((i + 1) * nb8, last8), 0)),
        _const_spec((1, d)), _const_spec((d, 2 * D_FF)), _const_spec((3, D_FF)), _const_spec((1, D_FF)),
        _const_spec((D_FF, d)),
    ]
    return pl.pallas_call(
        functools.partial(_ffn_kernel, tm=tm), grid=(b, s // tm), in_specs=in_specs, out_specs=row,
        out_shape=jax.ShapeDtypeStruct((b, s, d), F32),
        compiler_params=_params(2), name="conv_ffn",
    )(x, x, x, g[None].astype(F32), w_in.astype(BF16), conv_w.astype(F32), conv_b[None].astype(F32),
      w_out.astype(BF16))


def _rope_tables(s):
    pos = jnp.arange(s)
    row = pos // GRID_W
    col = pos - row * GRID_W
    inv = ROPE_THETA ** (-jnp.arange(0, DIFF_QK_DIM, 2, dtype=F32) / DIFF_QK_DIM)
    out = []
    for p in (pos, row, col):
        ang = inv[:, None] * p.astype(F32)[None, :]
        out += [jnp.cos(ang), jnp.sin(ang)]
    return jnp.stack(out)


def kernel(x, mem, norm_mix, w_in, ml_bias_i, ml_bias_f, ml_norm, pool_w, pool_scale, diff_qnorm, diff_knorm, diff_lambda, diff_subnorm, gqa_qnorm, gqa_knorm, w_out, norm_cross, norm_mem, w_cq, w_ckv, cross_qnorm, cross_knorm, w_co, norm_ffn, w_ffn_in, ffn_conv, ffn_conv_b, w_ffn_out):
    depth = w_in.shape[0]
    tabs = _rope_tables(x.shape[1])
    for l in range(depth):
        lam_init = 0.8 - 0.6 * math.exp(-0.3 * l)
        w_rm, w_gate, w_t = _inproj_weights(w_in[l])
        (rm, gates, gates_t, qtd, kd, vtd, qtg, kg, vtg, mkt) = _in_projection(
            x, norm_mix[l], w_rm, w_gate, w_t, tabs, diff_qnorm[l], diff_knorm[l], gqa_qnorm[l], gqa_knorm[l])
        hf, hb = _mlstm(rm, mkt, gates, gates_t, ml_bias_i[l], ml_bias_f[l])
        yc = _diff_attention(qtd, kd, vtd, diff_lambda[l], diff_subnorm[l], lam_init)
        yd = _gqa_attention(qtg, kg, vtg)
        x = _out_projection(hf, hb, rm, yc, yd, x, ml_norm[l], pool_w[l], pool_scale[l], w_out[l])
        ck, cv = _mem_kv(mem, norm_mem[l], w_ckv[l], cross_knorm[l])
        x = _cross_attention(x, norm_cross[l], w_cq[l], cross_qnorm[l], ck, cv, w_co[l])
        x = _conv_ffn(x, norm_ffn[l], w_ffn_in[l], ffn_conv[l], ffn_conv_b[l], w_ffn_out[l])
    return x
```

```python
import functools
import math

import jax
import jax.numpy as jnp
from jax import lax
from jax.experimental import pallas as pl
from jax.experimental.pallas import tpu as pltpu

F32 = jnp.float32
BF16 = jnp.bfloat16

D_MODEL = 1024
HEAD_DIM = 64
GROUP_WIDTH = 256
N_HEADS = 4
POOL_WINDOWS = (2, 4, 8, 16)
DIFF_QK_DIM = 32
GQA_KV_HEADS = 2
GRID_W = 64
ROPE_THETA = 10000.0
CROSS_HEADS = 4
CROSS_HEAD_DIM = 256
D_FF = 2816
EPS = 1e-6

LANES = 128
SUBLANES = 8
VMEM_LIMIT_BYTES = 56 * 1024 * 1024

V_AUG = HEAD_DIM + 16
ML_CHUNK = 128
NEG_BIG = -1e30
LOG2E = math.log2(math.e)
MAX_SCORE_BOUND = 50.0

_NT = (((1,), (1,)), ((), ()))


def _dot(a, b):
    return jnp.dot(a, b, preferred_element_type=F32)


def _dot_nt(a, b):
    return lax.dot_general(a, b, _NT, preferred_element_type=F32)


def _dot_f32(a, b):
    return jnp.dot(a, b, preferred_element_type=F32, precision=lax.Precision.HIGHEST)


def _params(n_axes):
    return pltpu.CompilerParams(dimension_semantics=("arbitrary",) * n_axes,
                                vmem_limit_bytes=VMEM_LIMIT_BYTES)


def _const_spec(shape):
    nd = len(shape)
    return pl.BlockSpec(shape, lambda *_: (0,) * nd, pipeline_mode=pl.Buffered(1))


def _rms(x, g):
    ms = jnp.mean(x * x, axis=-1, keepdims=True)
    return x * lax.rsqrt(ms + EPS) * g


RM_Q, RM_K, RM_V, RM_O, RM_POOL = 0, 512, 1024, 1536, 2048
RM_COLS = 2304
T_DQ, T_DK, T_GQ, T_GK, T_DV, T_GV, T_MK, T_GATE = 0, 256, 512, 768, 896, 1152, 1280, 1792
T_ROWS = 1808


def _norm_rope_t(z, norm_rows, gain, cos64, sin64, scale):
    r, t = z.shape
    zg = z.reshape(r // norm_rows, norm_rows, t)
    ms = jnp.mean(zg * zg, axis=1, keepdims=True)
    zn = (zg * lax.rsqrt(ms + EPS) * gain[None]).reshape(r // 64, 64, t)
    sw = jnp.concatenate([zn[:, 16:32], zn[:, 0:16], zn[:, 48:64], zn[:, 32:48]], axis=1)
    out = zn * cos64[None] + sw * sin64[None]
    return (out * scale).reshape(r, t)


def _with_ones_rows(vt, heads):
    t = vt.shape[1]
    v3 = vt.reshape(heads, HEAD_DIM, t)
    ones = jnp.ones((heads, V_AUG - HEAD_DIM, t), F32)
    return jnp.concatenate([v3, ones], axis=1).reshape(heads * V_AUG, t)


def _inproj_kernel(x_ref, g_ref, wrm_ref, wg_ref, wt_ref, cscale_ref, cadd_ref, tabs_ref,
                   gqd_ref, gkd_ref, gqg_ref, gkg_ref,
                   rm_ref, gates_ref, gates_t_ref, qtd_ref, kd_ref, vtd_ref,
                   qtg_ref, kg_ref, vtg_ref, mkt_ref):
    x = x_ref[0]
    h = _rms(x, g_ref[...]).astype(BF16)
    zr = _dot(h, wrm_ref[...])
    rm_ref[0] = (zr * cscale_ref[...] + cadd_ref[...]).astype(BF16)
    gates_ref[0] = _dot(h, wg_ref[...])
    zt = _dot_nt(wt_ref[...], h)

    tabs = tabs_ref[...]
    c1, s1, rc, rs, cc, cs = (tabs[i] for i in range(6))
    cos_d = jnp.concatenate([c1, c1, c1, c1], axis=0)
    sin_d = jnp.concatenate([-s1, s1, -s1, s1], axis=0)
    cos_g = jnp.concatenate([rc, rc, cc, cc], axis=0)
    sin_g = jnp.concatenate([-rs, rs, -cs, cs], axis=0)

    qd = _norm_rope_t(zt[T_DQ:T_DQ + 256], DIFF_QK_DIM, gqd_ref[...], cos_d, sin_d,
                      DIFF_QK_DIM ** -0.5 * LOG2E)
    qtd_ref[0] = qd.astype(BF16)
    kd = _norm_rope_t(zt[T_DK:T_DK + 256], DIFF_QK_DIM, gkd_ref[...], cos_d, sin_d, 1.0)
    kd_ref[0] = kd.T.astype(BF16)
    qg = _norm_rope_t(zt[T_GQ:T_GQ + 256], HEAD_DIM, gqg_ref[...], cos_g, sin_g, HEAD_DIM ** -0.5 * LOG2E)
    qtg_ref[0] = qg.astype(BF16)
    kg = _norm_rope_t(zt[T_GK:T_GK + 128], HEAD_DIM, gkg_ref[...], cos_g, sin_g, 1.0)
    kg_ref[0] = kg.T.astype(BF16)
    vtd_ref[0] = _with_ones_rows(zt[T_DV:T_DV + 256], N_HEADS).astype(BF16)
    vtg_ref[0] = _with_ones_rows(zt[T_GV:T_GV + 128], GQA_KV_HEADS).astype(BF16)
    mkt_ref[0] = (zt[T_MK:T_MK + 512] * (HEAD_DIM ** -0.5)).astype(BF16)
    gates_t_ref[0] = zt[T_GATE:T_GATE + 16]


def _pad_heads_cols(w):
    d = w.shape[0]
    w4 = w.reshape(d, N_HEADS, HEAD_DIM)
    return jnp.concatenate([w4, jnp.zeros_like(w4)], axis=2).reshape(d, N_HEADS * LANES)


def _inproj_weights(w_in):
    sizes = (256, 256, 256, 256, 16, 256, 256, 256, 256, 256, 128, 128)
    offs = [0]
    for s in sizes:
        offs.append(offs[-1] + s)
    (ml_q, ml_k, ml_v, ml_o, ml_g, pool, d_q, d_k, d_v, g_q, g_k, g_v) = (
        w_in[:, offs[i]:offs[i + 1]] for i in range(12))
    w_rm = jnp.concatenate([_pad_heads_cols(ml_q), _pad_heads_cols(ml_k), _pad_heads_cols(ml_v),
                            _pad_heads_cols(ml_o), pool], axis=1).astype(BF16)
    w_gate = jnp.concatenate([ml_g, jnp.zeros((D_MODEL, LANES - 16), F32)], axis=1).astype(BF16)
    w_t = jnp.concatenate([d_q, d_k, g_q, g_k, d_v, g_v, _pad_heads_cols(ml_k), ml_g], axis=1).T.astype(BF16)
    return w_rm, w_gate, w_t


def _inproj_col_consts():
    lane = jnp.arange(RM_COLS)
    in_k = (lane >= RM_K) & (lane < RM_V)
    cscale = jnp.where(in_k, HEAD_DIM ** -0.5, 1.0).astype(F32)[None]
    ones_col = (lane >= RM_V) & (lane < RM_O) & ((lane % LANES) == HEAD_DIM)
    cadd = jnp.where(ones_col, 1.0, 0.0).astype(F32)[None]
    return cscale, cadd


def _in_projection(x, g, w_rm, w_gate, w_t, tabs, gqd, gkd, gqg, gkg, tm=512):
    b, s, d = x.shape
    cscale, cadd = _inproj_col_consts()
    bc = lambda v: jnp.broadcast_to(v.astype(F32)[:, None], (v.shape[0], tm))
    grid = (b, s // tm)
    row_blk = lambda w: pl.BlockSpec((1, tm, w), lambda bi, i: (bi, i, 0))
    col_blk = lambda r: pl.BlockSpec((1, r, tm), lambda bi, i: (bi, 0, i))
    out_shapes = (
        jax.ShapeDtypeStruct((b, s, RM_COLS), BF16),
        jax.ShapeDtypeStruct((b, s, LANES), F32),
        jax.ShapeDtypeStruct((b, 16, s), F32),
        jax.ShapeDtypeStruct((b, 256, s), BF16),
        jax.ShapeDtypeStruct((b, s, 256), BF16),
        jax.ShapeDtypeStruct((b, N_HEADS * V_AUG, s), BF16),
        jax.ShapeDtypeStruct((b, 256, s), BF16),
        jax.ShapeDtypeStruct((b, s, 128), BF16),
        jax.ShapeDtypeStruct((b, GQA_KV_HEADS * V_AUG, s), BF16),
        jax.ShapeDtypeStruct((b, 512, s), BF16),
    )
    out_specs = (row_blk(RM_COLS), row_blk(LANES), col_blk(16), col_blk(256), row_blk(256),
                 col_blk(N_HEADS * V_AUG), col_blk(256), row_blk(128), col_blk(GQA_KV_HEADS * V_AUG),
                 col_blk(512))
    in_specs = [
        row_blk(d),
        _const_spec((1, d)),
        _const_spec(w_rm.shape), _const_spec(w_gate.shape), _const_spec(w_t.shape),
        _const_spec((1, RM_COLS)), _const_spec((1, RM_COLS)),
        pl.BlockSpec((6, 16, tm), lambda bi, i: (0, 0, i)),
        _const_spec((DIFF_QK_DIM, tm)), _const_spec((DIFF_QK_DIM, tm)),
        _const_spec((HEAD_DIM, tm)), _const_spec((HEAD_DIM, tm)),
    ]
    return pl.pallas_call(
        _inproj_kernel, grid=grid, in_specs=in_specs, out_specs=out_specs, out_shape=out_shapes,
        compiler_params=_params(2), name="in_projection",
    )(x, g[None].astype(F32), w_rm, w_gate, w_t, cscale, cadd, tabs, bc(gqd), bc(gkd), bc(gqg), bc(gkg))


def _attn_kernel(*refs, combos, width, tq, tk, n_extra, finalize, bounded):
    shift_ref, qt_ref, k_ref, vt_ref = refs[:4]
    extra = refs[4:4 + n_extra]
    o_ref = refs[4 + n_extra]
    qpad_sc, m_sc, acc_sc, p_sc = refs[5 + n_extra:]
    n_kv = k_ref.shape[1] // tk
    n_c = len(combos)

    for c, (q_row0, q_rows, k_off, _) in enumerate(combos):
        pieces = []
        if k_off:
            pieces.append(jnp.zeros((k_off, tq), BF16))
        pieces.append(qt_ref[0, q_row0:q_row0 + q_rows, :])
        if width - k_off - q_rows:
            pieces.append(jnp.zeros((width - k_off - q_rows, tq), BF16))
        qpad_sc[c] = jnp.concatenate(pieces, axis=0) if len(pieces) > 1 else pieces[0]
    m_sc[...] = jnp.full(m_sc.shape, NEG_BIG, F32)
    acc_sc[...] = jnp.zeros(acc_sc.shape, F32)
    shift = shift_ref[0, 0]

    def scores(c, start):
        return _dot(k_ref[0, pl.ds(start, tk), :], qpad_sc[c])

    def values(c, start):
        v_row0 = combos[c][3]
        return vt_ref[0, v_row0:v_row0 + V_AUG, pl.ds(start, tk)]

    def bounded_body(j, carry):
        start = pl.multiple_of(j * tk, tk)
        for c in range(n_c):
            p_sc[c] = jnp.exp2(scores(c, start) - shift).astype(BF16)
        for c in range(n_c):
            acc_sc[c] += _dot(values(c, start), p_sc[c])
        return carry

    def online_body(j, carry):
        start = pl.multiple_of(j * tk, tk)
        for c in range(n_c):
            st = scores(c, start)
            m_prev = m_sc[c]
            m_new = jnp.maximum(m_prev, jnp.max(st, axis=0, keepdims=True))
            p = jnp.exp2(st - m_new).astype(BF16)
            acc_sc[c] = jnp.exp2(m_prev - m_new) * acc_sc[c] + _dot(values(c, start), p)
            m_sc[c] = m_new
        return carry

    lax.fori_loop(0, n_kv, bounded_body if bounded else online_body, 0)
    out_t = finalize([acc_sc[c] for c in range(len(combos))], extra)
    o_ref[0] = out_t.T.astype(BF16)


def _finalize_diff(accs, extra, *, lam_init):
    lam_ref, gsub_ref = extra
    lp = lam_ref[...]
    lam = (jnp.exp(jnp.sum(lp[0:1] * lp[1:2], axis=1, keepdims=True))
           - jnp.exp(jnp.sum(lp[2:3] * lp[3:4], axis=1, keepdims=True)) + lam_init)
    outs = []
    for h in range(N_HEADS):
        a1, a2 = accs[2 * h], accs[2 * h + 1]
        o = (a1[:HEAD_DIM] / a1[HEAD_DIM:HEAD_DIM + 1]
             - lam * (a2[:HEAD_DIM] / a2[HEAD_DIM:HEAD_DIM + 1]))
        ms = jnp.mean(o * o, axis=0, keepdims=True)
        outs.append(o * lax.rsqrt(ms + EPS) * gsub_ref[...] * (1.0 - lam_init))
    return jnp.concatenate(outs, axis=0)


def _finalize_gqa(accs, extra):
    return jnp.concatenate([a[:HEAD_DIM] / a[HEAD_DIM:HEAD_DIM + 1] for a in accs], axis=0)


def _score_bound(g_q, g_k, dim):
    return 1.02 * (dim ** 0.5) * LOG2E * jnp.max(jnp.abs(g_q)) * jnp.max(jnp.abs(g_k))


def _attention(qt, k, vt, extra, bound, *, combos, finalize, tq=256, tk=512):
    b, _, s = qt.shape
    width = k.shape[2]
    n_c = len(combos)
    in_specs = [
        pl.BlockSpec(memory_space=pltpu.SMEM),
        pl.BlockSpec((1, qt.shape[1], tq), lambda bi, i: (bi, 0, i)),
        pl.BlockSpec((1, s, width), lambda bi, i: (bi, 0, 0)),
        pl.BlockSpec((1, vt.shape[1], s), lambda bi, i: (bi, 0, 0)),
    ] + [_const_spec(e.shape) for e in extra]

    def call(bounded):
        kern = functools.partial(_attn_kernel, combos=combos, width=width, tq=tq, tk=tk,
                                 n_extra=len(extra), finalize=finalize, bounded=bounded)
        return pl.pallas_call(
            kern, grid=(b, s // tq), in_specs=in_specs,
            out_specs=pl.BlockSpec((1, tq, 256), lambda bi, i: (bi, i, 0)),
            out_shape=jax.ShapeDtypeStruct((b, s, 256), BF16),
            scratch_shapes=[pltpu.VMEM((n_c, width, tq), BF16),
                            pltpu.VMEM((n_c, 1, tq), F32),
                            pltpu.VMEM((n_c, V_AUG, tq), F32),
                            pltpu.VMEM((n_c, tk, tq) if bounded else (1, 16, LANES), BF16)],
            compiler_params=_params(2),
            name="attention_%d_%s" % (width, "bounded" if bounded else "online"),
        )(bound.reshape(1, 1).astype(F32), qt, k, vt, *extra)

    return lax.cond(bound <= MAX_SCORE_BOUND, lambda: call(True), lambda: call(False))


def _diff_attention(qt, k, vt, lam_params, g_sub, lam_init, bound, tq=512):
    combos = tuple((64 * h + 32 * c, 32, 64 * h + 32 * c, V_AUG * h)
                   for h in range(N_HEADS) for c in range(2))
    gsub_b = jnp.broadcast_to(g_sub.astype(F32)[:, None], (HEAD_DIM, tq))
    fin = functools.partial(_finalize_diff, lam_init=lam_init)
    return _attention(qt, k, vt, (lam_params.astype(F32), gsub_b), bound, combos=combos, finalize=fin, tq=tq)


def _gqa_attention(qt, k, vt, bound, tq=512):
    combos = tuple((64 * qh, 64, 64 * (qh // 2), V_AUG * (qh // 2)) for qh in range(N_HEADS))
    return _attention(qt, k, vt, (), bound, combos=combos, finalize=_finalize_gqa, tq=tq)


def _log_sigmoid(x):
    return jnp.minimum(x, 0.0) - jnp.log(1.0 + jnp.exp(-jnp.abs(x)))


def _mlstm_kernel(qf_ref, kf_ref, vf_ref, ktf_ref, gf_ref, gtf_ref,
                  qb_ref, kb_ref, vb_ref, ktb_ref, gb_ref, gtb_ref,
                  brow_ref, bcol_ref, hf_ref, hb_ref, c_sc, m_sc):
    L = ML_CHUNK

    @pl.when(pl.program_id(1) == 0)
    def _():
        c_sc[...] = jnp.zeros(c_sc.shape, F32)
        m_sc[...] = jnp.zeros(m_sc.shape, F32)

    row = lax.broadcasted_iota(jnp.int32, (L, L), 0)
    col = lax.broadcasted_iota(jnp.int32, (L, L), 1)
    lower = col <= row
    upper = col >= row
    lower_f = lower.astype(F32)
    upper_f = upper.astype(F32)
    lane = lax.broadcasted_iota(jnp.int32, (1, LANES), 1)
    forget_lane = (lane % 8) >= 4
    grow = lax.broadcasted_iota(jnp.int32, (16, 1), 0)
    forget_row = (grow % 8) >= 4
    keep = (lax.broadcasted_iota(jnp.int32, (1, LANES), 1) < HEAD_DIM).astype(F32)

    dirs = (
        (0, qf_ref, kf_ref, vf_ref, ktf_ref, gf_ref, gtf_ref, hf_ref, lower, lower_f, upper_f, L - 1),
        (1, qb_ref, kb_ref, vb_ref, ktb_ref, gb_ref, gtb_ref, hb_ref, upper, upper_f, lower_f, 0),
    )
    for d, q_ref, k_ref, v_ref, kt_ref, g_ref, gt_ref, h_ref, mask, tri_c, tri_r, last in dirs:
        g = g_ref[0] + brow_ref[...]
        gp = jnp.where(forget_lane, _log_sigmoid(g), g)
        gt = gt_ref[0] + bcol_ref[...]
        gtp = jnp.where(forget_row, _log_sigmoid(gt), gt)
        bcol_all = _dot_f32(tri_c, gp)
        brow_all = _dot_f32(gtp, tri_r)
        for h in range(N_HEADS):
            ii, fi = d * 8 + h, d * 8 + 4 + h
            idx = d * N_HEADS + h
            li_col = gp[:, ii:ii + 1]
            b_col = bcol_all[:, fi:fi + 1]
            li_row = gtp[ii:ii + 1, :]
            b_row = brow_all[fi:fi + 1, :]
            g_tot = b_row[:, last:last + 1]
            m_prev = m_sc[idx][0:1, 0:1]

            qh = q_ref[0, :, h * LANES:(h + 1) * LANES]
            kh = k_ref[0, :, h * LANES:(h + 1) * LANES]
            vh = v_ref[0, :, h * LANES:(h + 1) * LANES]
            kth = kt_ref[0, h * LANES:(h + 1) * LANES, :]

            dlog = jnp.where(mask, b_col - b_row + li_row, -jnp.inf)
            e_col = b_col + m_prev
            m_t = jnp.maximum(e_col, jnp.max(dlog, axis=1, keepdims=True))
            wts = (jnp.exp(dlog - m_t) * _dot_nt(qh, kh)).astype(BF16)
            s_inter = jnp.exp(e_col - m_t)
            c_prev = c_sc[idx]
            tot = _dot(wts, vh) + s_inter * _dot(qh, c_prev.astype(BF16))
            den = tot[:, HEAD_DIM:HEAD_DIM + 1]
            hout = tot / jnp.maximum(jnp.abs(den), jnp.exp(-m_t))
            h_ref[0, :, h * LANES:(h + 1) * LANES] = (hout * keep).astype(BF16)

            a_row = g_tot - b_row + li_row
            m_loc = jnp.max(a_row, axis=1, keepdims=True)
            w_col = jnp.exp(g_tot - b_col + li_col - m_loc)
            c_loc = _dot(kth, (vh.astype(F32) * w_col).astype(BF16))
            m_new = jnp.maximum(g_tot + m_prev, m_loc)
            f_old = jnp.exp(g_tot + m_prev - m_new)
            f_loc = jnp.exp(m_loc - m_new)
            c_sc[idx] = f_old * c_prev + f_loc * c_loc
            m_sc[idx] = jnp.broadcast_to(m_new, (SUBLANES, LANES))


def _mlstm(rm, mkt, gates, gates_t, b_i, b_f):
    b, s, _ = rm.shape
    L = ML_CHUNK
    nc = s // L
    bias = jnp.stack([b_i[0], b_f[0], b_i[1], b_f[1]]).reshape(16).astype(F32)
    bias_row = jnp.concatenate([bias, jnp.zeros((LANES - 16,), F32)])[None]
    bias_col = jnp.broadcast_to(bias[:, None], (16, L))
    fwd = lambda cblk: (lambda bi, i: (bi, i, cblk))
    bwd = lambda cblk: (lambda bi, i: (bi, nc - 1 - i, cblk))
    in_specs = []
    for mk in (fwd, bwd):
        in_specs += [pl.BlockSpec((1, L, 512), mk(RM_Q // 512)),
                     pl.BlockSpec((1, L, 512), mk(RM_K // 512)),
                     pl.BlockSpec((1, L, 512), mk(RM_V // 512)),
                     pl.BlockSpec((1, 512, L), (lambda bi, i: (bi, 0, i)) if mk is fwd
                                  else (lambda bi, i: (bi, 0, nc - 1 - i))),
                     pl.BlockSpec((1, L, LANES), mk(0)),
                     pl.BlockSpec((1, 16, L), (lambda bi, i: (bi, 0, i)) if mk is fwd
                                  else (lambda bi, i: (bi, 0, nc - 1 - i)))]
    in_specs += [_const_spec((1, LANES)), _const_spec((16, L))]
    out_specs = (pl.BlockSpec((1, L, 512), lambda bi, i: (bi, i, 0)),
                 pl.BlockSpec((1, L, 512), lambda bi, i: (bi, nc - 1 - i, 0)))
    out_shape = (jax.ShapeDtypeStruct((b, s, 512), BF16), jax.ShapeDtypeStruct((b, s, 512), BF16))
    return pl.pallas_call(
        _mlstm_kernel, grid=(b, nc), in_specs=in_specs, out_specs=out_specs, out_shape=out_shape,
        scratch_shapes=[pltpu.VMEM((2 * N_HEADS, LANES, LANES), F32),
                        pltpu.VMEM((2 * N_HEADS, SUBLANES, LANES), F32)],
        compiler_params=_params(2), name="mlstm",
    )(rm, rm, rm, mkt, gates, gates_t, rm, rm, rm, mkt, gates, gates_t, bias_row, bias_col)


def _outproj_kernel(hf_ref, hb_ref, o_ref, pm_ref, pp_ref, pn_ref, yc_ref, yd_ref, x_ref,
                    gml_ref, wpool_ref, pscale_ref, wa_ref, wb_ref, wc_ref, wd_ref, out_ref, *, tm, seq):
    i = pl.program_id(1)
    n_t = pl.num_programs(1)

    hs = hf_ref[0].astype(F32) + hb_ref[0].astype(F32)
    og = o_ref[0].astype(F32)
    parts = []
    for h in range(N_HEADS):
        sl = slice(h * LANES, (h + 1) * LANES)
        hh = hs[:, sl]
        ms = jnp.sum(hh * hh, axis=-1, keepdims=True) * (1.0 / HEAD_DIM)
        gate = 1.0 / (1.0 + jnp.exp(-og[:, sl]))
        parts.append(hh * lax.rsqrt(ms + EPS) * gml_ref[:, sl] * gate)
    ya = jnp.concatenate(parts, axis=1).astype(BF16)

    prev = jnp.where(i > 0, pp_ref[0].astype(F32), 0.0)
    nxt = jnp.where(i < n_t - 1, pn_ref[0].astype(F32), 0.0)
    u = jnp.concatenate([prev, pm_ref[0].astype(F32), nxt], axis=0)
    n = tm + 2 * SUBLANES
    sh = lambda v, k: pltpu.roll(v, k % n, 0)
    w2 = u + sh(u, 1)
    w4 = sh(w2, 1) + sh(w2, -1)
    w8 = sh(w4, 2) + sh(w4, -2)
    w16 = sh(w8, 4) + sh(w8, -4)
    core = slice(SUBLANES, SUBLANES + tm)
    lane = lax.broadcasted_iota(jnp.int32, (1, GROUP_WIDTH), 1)
    t = (i * tm + lax.broadcasted_iota(jnp.int32, (tm, 1), 0)).astype(F32)
    total = jnp.where(lane < 64, w2[core], jnp.where(lane < 128, w4[core],
                      jnp.where(lane < 192, w8[core], w16[core])))
    half = jnp.where(lane < 64, 1.0, jnp.where(lane < 128, 2.0, jnp.where(lane < 192, 4.0, 8.0)))
    lo = jnp.maximum(t - half, 0.0)
    hi = jnp.minimum(t + half - 1.0, seq - 1.0)
    pooled = total / (hi - lo + 1.0) - u[core]
    yb = (_dot(pooled.astype(BF16), wpool_ref[...]) * pscale_ref[...]).astype(BF16)

    out_ref[0] = (x_ref[0] + _dot(ya, wa_ref[...]) + _dot(yb, wb_ref[...])
                  + _dot(yc_ref[0], wc_ref[...]) + _dot(yd_ref[0], wd_ref[...]))


def _out_projection(hf, hb, rm, yc, yd, x, ml_norm, pool_w, pool_scale, w_out, tm=512):
    b, s, d = x.shape
    wa = w_out[0:256].reshape(N_HEADS, HEAD_DIM, d)
    wa = jnp.concatenate([wa, jnp.zeros_like(wa)], axis=1).reshape(N_HEADS * LANES, d).astype(BF16)
    wb, wc, wd = (w_out[256 * j:256 * (j + 1)].astype(BF16) for j in (1, 2, 3))
    gml = ml_norm.reshape(N_HEADS, HEAD_DIM)
    gml = jnp.concatenate([gml, jnp.zeros_like(gml)], axis=1).reshape(1, N_HEADS * LANES).astype(F32)
    wpool = jax.scipy.linalg.block_diag(*[pool_w[g] for g in range(4)]).astype(BF16)
    nb8 = tm // SUBLANES
    last8 = s // SUBLANES - 1
    row = lambda w, cb=0: pl.BlockSpec((1, tm, w), lambda bi, i: (bi, i, cb))
    in_specs = [
        row(512), row(512), row(512, RM_O // 512),
        row(256, RM_POOL // 256),
        pl.BlockSpec((1, SUBLANES, 256), lambda bi, i: (bi, jnp.maximum(i * nb8 - 1, 0), RM_POOL // 256)),
        pl.BlockSpec((1, SUBLANES, 256), lambda bi, i: (bi, jnp.minimum((i + 1) * nb8, last8), RM_POOL // 256)),
        row(256), row(256), row(d),
        _const_spec((1, 512)), _const_spec((256, 256)), _const_spec((1, 256)),
        _const_spec((512, d)), _const_spec((256, d)), _const_spec((256, d)), _const_spec((256, d)),
    ]
    kern = functools.partial(_outproj_kernel, tm=tm, seq=s)
    return pl.pallas_call(
        kern, grid=(b, s // tm), in_specs=in_specs, out_specs=row(d),
        out_shape=jax.ShapeDtypeStruct((b, s, d), F32),
        compiler_params=_params(2), name="out_projection",
    )(hf, hb, rm, rm, rm, rm, yc, yd, x, gml, wpool, pool_scale[None].astype(F32), wa, wb, wc, wd)


def _memkv_kernel(mem_ref, g_ref, w_ref, gk_ref, k_ref, v_ref):
    mn = _rms(mem_ref[0], g_ref[...]).astype(BF16)
    kv = _dot(mn, w_ref[...])
    ks = []
    for h in range(CROSS_HEADS):
        ks.append(_rms(kv[:, h * CROSS_HEAD_DIM:(h + 1) * CROSS_HEAD_DIM], gk_ref[...]))
    k_ref[0] = jnp.concatenate(ks, axis=1).astype(BF16)
    v_ref[0] = kv[:, D_MODEL:].astype(BF16)


def _mem_kv(mem, g_mem, w_kv, g_k):
    b, m, d = mem.shape
    blk = pl.BlockSpec((1, m, d), lambda bi: (bi, 0, 0))
    return pl.pallas_call(
        _memkv_kernel, grid=(b,),
        in_specs=[blk, _const_spec((1, d)), _const_spec((d, 2 * d)), _const_spec((1, CROSS_HEAD_DIM))],
        out_specs=(blk, blk),
        out_shape=(jax.ShapeDtypeStruct((b, m, d), BF16), jax.ShapeDtypeStruct((b, m, d), BF16)),
        compiler_params=_params(1), name="mem_kv",
    )(mem, g_mem[None].astype(F32), w_kv.astype(BF16), g_k[None].astype(F32))


def _cross_kernel(x_ref, g_ref, wq_ref, gq_ref, k_ref, v_ref, wo_ref, out_ref):
    x = x_ref[0]
    xn = _rms(x, g_ref[...]).astype(BF16)
    q = _dot(xn, wq_ref[...])
    outs = []
    for h in range(CROSS_HEADS):
        sl = slice(h * CROSS_HEAD_DIM, (h + 1) * CROSS_HEAD_DIM)
        qh = (_rms(q[:, sl], gq_ref[...]) * (CROSS_HEAD_DIM ** -0.5)).astype(BF16)
        s = _dot_nt(qh, k_ref[0, :, sl])
        p = jnp.exp(s - jnp.max(s, axis=-1, keepdims=True))
        l = jnp.sum(p, axis=-1, keepdims=True)
        outs.append((_dot(p.astype(BF16), v_ref[0, :, sl]) / l).astype(BF16))
    o = jnp.concatenate(outs, axis=1)
    out_ref[0] = x + _dot(o, wo_ref[...])


def _cross_attention(x, g, w_q, g_q, k, v, w_o, tm=512):
    b, s, d = x.shape
    m = k.shape[1]
    row = pl.BlockSpec((1, tm, d), lambda bi, i: (bi, i, 0))
    kvb = pl.BlockSpec((1, m, d), lambda bi, i: (bi, 0, 0))
    return pl.pallas_call(
        _cross_kernel, grid=(b, s // tm),
        in_specs=[row, _const_spec((1, d)), _const_spec((d, d)), _const_spec((1, CROSS_HEAD_DIM)),
                  kvb, kvb, _const_spec((d, d))],
        out_specs=row, out_shape=jax.ShapeDtypeStruct((b, s, d), F32),
        compiler_params=_params(2), name="cross_attention",
    )(x, g[None].astype(F32), w_q.astype(BF16), g_q[None].astype(F32), k, v, w_o.astype(BF16))


FF_CHUNK = 256


def _ffn_kernel(xm_ref, xp_ref, xn_ref, g_ref, win_ref, cw_ref, cb_ref, wout_ref, out_ref, *, tm):
    i = pl.program_id(1)
    n_t = pl.num_programs(1)
    xm = xm_ref[0]
    xe = jnp.concatenate([xp_ref[0], xm, xn_ref[0]], axis=0)
    hn = _rms(xe, g_ref[...]).astype(BF16)
    n = tm + 2 * SUBLANES
    core = slice(SUBLANES, SUBLANES + tm)
    hc = hn[core]
    r = lax.broadcasted_iota(jnp.int32, (n, 1), 0)
    live = jnp.logical_and(jnp.logical_or(i > 0, r >= SUBLANES),
                           jnp.logical_or(i < n_t - 1, r < SUBLANES + tm)).astype(F32)
    acc = xm
    for c in range(D_FF // FF_CHUNK):
        gs = slice(c * FF_CHUNK, (c + 1) * FF_CHUNK)
        us = slice(D_FF + c * FF_CHUNK, D_FF + (c + 1) * FF_CHUNK)
        gate = _dot(hn, win_ref[:, gs]) * live
        cw = cw_ref[:, gs]
        gconv = (pltpu.roll(gate, 1, 0)[core] * cw[0:1] + gate[core] * cw[1:2]
                 + pltpu.roll(gate, n - 1, 0)[core] * cw[2:3] + cb_ref[:, gs])
        up = _dot(hc, win_ref[:, us])
        act = (gconv / (1.0 + jnp.exp(-gconv)) * up).astype(BF16)
        acc = acc + _dot(act, wout_ref[gs, :])
    out_ref[0] = acc


def _conv_ffn(x, g, w_in, conv_w, conv_b, w_out, tm=256):
    b, s, d = x.shape
    nb8 = tm // SUBLANES
    last8 = s // SUBLANES - 1
    row = pl.BlockSpec((1, tm, d), lambda bi, i: (bi, i, 0))
    in_specs = [
        row,
        pl.BlockSpec((1, SUBLANES, d), lambda bi, i: (bi, jnp.maximum(i * nb8 - 1, 0), 0)),
        pl.BlockSpec((1, SUBLANES, d), lambda bi, i: (bi, jnp.minimum((i + 1) * nb8, last8), 0)),
        _const_spec((1, d)), _const_spec((d, 2 * D_FF)), _const_spec((3, D_FF)), _const_spec((1, D_FF)),
        _const_spec((D_FF, d)),
    ]
    return pl.pallas_call(
        functools.partial(_ffn_kernel, tm=tm), grid=(b, s // tm), in_specs=in_specs, out_specs=row,
        out_shape=jax.ShapeDtypeStruct((b, s, d), F32),
        compiler_params=_params(2), name="conv_ffn",
    )(x, x, x, g[None].astype(F32), w_in.astype(BF16), conv_w.astype(F32), conv_b[None].astype(F32),
      w_out.astype(BF16))


def _rope_tables(s):
    pos = jnp.arange(s)
    row = pos // GRID_W
    col = pos - row * GRID_W
    inv = ROPE_THETA ** (-jnp.arange(0, DIFF_QK_DIM, 2, dtype=F32) / DIFF_QK_DIM)
    out = []
    for p in (pos, row, col):
        ang = inv[:, None] * p.astype(F32)[None, :]
        out += [jnp.cos(ang), jnp.sin(ang)]
    return jnp.stack(out)


def kernel(x, mem, norm_mix, w_in, ml_bias_i, ml_bias_f, ml_norm, pool_w, pool_scale, diff_qnorm, diff_knorm, diff_lambda, diff_subnorm, gqa_qnorm, gqa_knorm, w_out, norm_cross, norm_mem, w_cq, w_ckv, cross_qnorm, cross_knorm, w_co, norm_ffn, w_ffn_in, ffn_conv, ffn_conv_b, w_ffn_out):
    depth = w_in.shape[0]
    tabs = _rope_tables(x.shape[1])
    for l in range(depth):
        lam_init = 0.8 - 0.6 * math.exp(-0.3 * l)
        w_rm, w_gate, w_t = _inproj_weights(w_in[l])
        (rm, gates, gates_t, qtd, kd, vtd, qtg, kg, vtg, mkt) = _in_projection(
            x, norm_mix[l], w_rm, w_gate, w_t, tabs, diff_qnorm[l], diff_knorm[l], gqa_qnorm[l], gqa_knorm[l])
        hf, hb = _mlstm(rm, mkt, gates, gates_t, ml_bias_i[l], ml_bias_f[l])
        yc = _diff_attention(qtd, kd, vtd, diff_lambda[l], diff_subnorm[l], lam_init,
                             _score_bound(diff_qnorm[l], diff_knorm[l], DIFF_QK_DIM))
        yd = _gqa_attention(qtg, kg, vtg, _score_bound(gqa_qnorm[l], gqa_knorm[l], HEAD_DIM))
        x = _out_projection(hf, hb, rm, yc, yd, x, ml_norm[l], pool_w[l], pool_scale[l], w_out[l])
        ck, cv = _mem_kv(mem, norm_mem[l], w_ckv[l], cross_knorm[l])
        x = _cross_attention(x, norm_cross[l], w_cq[l], cross_qnorm[l], ck, cv, w_co[l])
        x = _conv_ffn(x, norm_ffn[l], w_ffn_in[l], ffn_conv[l], ffn_conv_b[l], w_ffn_out[l])
    return x
```

```python
import functools
import math

import jax
import jax.numpy as jnp
from jax import lax
from jax.experimental import pallas as pl
from jax.experimental.pallas import tpu as pltpu

F32 = jnp.float32
BF16 = jnp.bfloat16

D_MODEL = 1024
HEAD_DIM = 64
GROUP_WIDTH = 256
N_HEADS = 4
POOL_WINDOWS = (2, 4, 8, 16)
DIFF_QK_DIM = 32
GQA_KV_HEADS = 2
GRID_W = 64
ROPE_THETA = 10000.0
CROSS_HEADS = 4
CROSS_HEAD_DIM = 256
D_FF = 2816
EPS = 1e-6

LANES = 128
SUBLANES = 8
VMEM_LIMIT_BYTES = 56 * 1024 * 1024

V_AUG = HEAD_DIM + 16
ML_CHUNK = 128
NEG_BIG = -1e30
LOG2E = math.log2(math.e)
MAX_SCORE_BOUND = 50.0

_NT = (((1,), (1,)), ((), ()))


def _dot(a, b):
    return jnp.dot(a, b, preferred_element_type=F32)


def _dot_nt(a, b):
    return lax.dot_general(a, b, _NT, preferred_element_type=F32)


def _dot_f32(a, b):
    return jnp.dot(a, b, preferred_element_type=F32, precision=lax.Precision.HIGHEST)


def _params(n_axes):
    return pltpu.CompilerParams(dimension_semantics=("arbitrary",) * n_axes,
                                vmem_limit_bytes=VMEM_LIMIT_BYTES)


def _const_spec(shape):
    nd = len(shape)
    return pl.BlockSpec(shape, lambda *_: (0,) * nd, pipeline_mode=pl.Buffered(1))


def _rms(x, g):
    ms = jnp.mean(x * x, axis=-1, keepdims=True)
    return x * lax.rsqrt(ms + EPS) * g


RM_Q, RM_K, RM_V, RM_O, RM_POOL = 0, 512, 1024, 1536, 2048
RM_COLS = 2304
T_DQ, T_DK, T_GQ, T_GK, T_DV, T_GV, T_MK, T_GATE = 0, 256, 512, 768, 896, 1152, 1280, 1792
T_ROWS = 1808


def _norm_rope_t(z, norm_rows, gain, cos64, sin64, scale):
    r, t = z.shape
    zg = z.reshape(r // norm_rows, norm_rows, t)
    ms = jnp.mean(zg * zg, axis=1, keepdims=True)
    zn = (zg * lax.rsqrt(ms + EPS) * gain[None]).reshape(r // 64, 64, t)
    sw = jnp.concatenate([zn[:, 16:32], zn[:, 0:16], zn[:, 48:64], zn[:, 32:48]], axis=1)
    out = zn * cos64[None] + sw * sin64[None]
    return (out * scale).reshape(r, t)


def _with_ones_rows(vt, heads):
    t = vt.shape[1]
    v3 = vt.reshape(heads, HEAD_DIM, t)
    ones = jnp.ones((heads, V_AUG - HEAD_DIM, t), F32)
    return jnp.concatenate([v3, ones], axis=1).reshape(heads * V_AUG, t)


def _inproj_kernel(x_ref, g_ref, wrm_ref, wg_ref, wt_ref, cscale_ref, cadd_ref, tabs_ref,
                   gqd_ref, gkd_ref, gqg_ref, gkg_ref,
                   rm_ref, gates_ref, gates_t_ref, qtd_ref, kd_ref, vtd_ref,
                   qtg_ref, kg_ref, vtg_ref, mkt_ref):
    x = x_ref[0]
    h = _rms(x, g_ref[...]).astype(BF16)
    zr = _dot(h, wrm_ref[...])
    rm_ref[0] = (zr * cscale_ref[...] + cadd_ref[...]).astype(BF16)
    gates_ref[0] = _dot(h, wg_ref[...])
    zt = _dot_nt(wt_ref[...], h)

    tabs = tabs_ref[...]
    c1, s1, rc, rs, cc, cs = (tabs[i] for i in range(6))
    cos_d = jnp.concatenate([c1, c1, c1, c1], axis=0)
    sin_d = jnp.concatenate([-s1, s1, -s1, s1], axis=0)
    cos_g = jnp.concatenate([rc, rc, cc, cc], axis=0)
    sin_g = jnp.concatenate([-rs, rs, -cs, cs], axis=0)

    qd = _norm_rope_t(zt[T_DQ:T_DQ + 256], DIFF_QK_DIM, gqd_ref[...], cos_d, sin_d,
                      DIFF_QK_DIM ** -0.5 * LOG2E)
    qtd_ref[0] = qd.astype(BF16)
    kd = _norm_rope_t(zt[T_DK:T_DK + 256], DIFF_QK_DIM, gkd_ref[...], cos_d, sin_d, 1.0)
    kd_ref[0] = kd.T.astype(BF16)
    qg = _norm_rope_t(zt[T_GQ:T_GQ + 256], HEAD_DIM, gqg_ref[...], cos_g, sin_g, HEAD_DIM ** -0.5 * LOG2E)
    qtg_ref[0] = qg.astype(BF16)
    kg = _norm_rope_t(zt[T_GK:T_GK + 128], HEAD_DIM, gkg_ref[...], cos_g, sin_g, 1.0)
    kg_ref[0] = kg.T.astype(BF16)
    vtd_ref[0] = _with_ones_rows(zt[T_DV:T_DV + 256], N_HEADS).astype(BF16)
    vtg_ref[0] = _with_ones_rows(zt[T_GV:T_GV + 128], GQA_KV_HEADS).astype(BF16)
    mkt_ref[0] = (zt[T_MK:T_MK + 512] * (HEAD_DIM ** -0.5)).astype(BF16)
    gates_t_ref[0] = zt[T_GATE:T_GATE + 16]


def _pad_heads_cols(w):
    d = w.shape[0]
    w4 = w.reshape(d, N_HEADS, HEAD_DIM)
    return jnp.concatenate([w4, jnp.zeros_like(w4)], axis=2).reshape(d, N_HEADS * LANES)


def _inproj_weights(w_in):
    sizes = (256, 256, 256, 256, 16, 256, 256, 256, 256, 256, 128, 128)
    offs = [0]
    for s in sizes:
        offs.append(offs[-1] + s)
    (ml_q, ml_k, ml_v, ml_o, ml_g, pool, d_q, d_k, d_v, g_q, g_k, g_v) = (
        w_in[:, offs[i]:offs[i + 1]] for i in range(12))
    w_rm = jnp.concatenate([_pad_heads_cols(ml_q), _pad_heads_cols(ml_k), _pad_heads_cols(ml_v),
                            _pad_heads_cols(ml_o), pool], axis=1).astype(BF16)
    w_gate = jnp.concatenate([ml_g, jnp.zeros((D_MODEL, LANES - 16), F32)], axis=1).astype(BF16)
    w_t = jnp.concatenate([d_q, d_k, g_q, g_k, d_v, g_v, _pad_heads_cols(ml_k), ml_g], axis=1).T.astype(BF16)
    return w_rm, w_gate, w_t


def _inproj_col_consts():
    lane = jnp.arange(RM_COLS)
    in_k = (lane >= RM_K) & (lane < RM_V)
    cscale = jnp.where(in_k, HEAD_DIM ** -0.5, 1.0).astype(F32)[None]
    ones_col = (lane >= RM_V) & (lane < RM_O) & ((lane % LANES) == HEAD_DIM)
    cadd = jnp.where(ones_col, 1.0, 0.0).astype(F32)[None]
    return cscale, cadd


def _in_projection(x, g, w_rm, w_gate, w_t, tabs, gqd, gkd, gqg, gkg, tm=512):
    b, s, d = x.shape
    cscale, cadd = _inproj_col_consts()
    bc = lambda v: jnp.broadcast_to(v.astype(F32)[:, None], (v.shape[0], tm))
    grid = (b, s // tm)
    row_blk = lambda w: pl.BlockSpec((1, tm, w), lambda bi, i: (bi, i, 0))
    col_blk = lambda r: pl.BlockSpec((1, r, tm), lambda bi, i: (bi, 0, i))
    out_shapes = (
        jax.ShapeDtypeStruct((b, s, RM_COLS), BF16),
        jax.ShapeDtypeStruct((b, s, LANES), F32),
        jax.ShapeDtypeStruct((b, 16, s), F32),
        jax.ShapeDtypeStruct((b, 256, s), BF16),
        jax.ShapeDtypeStruct((b, s, 256), BF16),
        jax.ShapeDtypeStruct((b, N_HEADS * V_AUG, s), BF16),
        jax.ShapeDtypeStruct((b, 256, s), BF16),
        jax.ShapeDtypeStruct((b, s, 128), BF16),
        jax.ShapeDtypeStruct((b, GQA_KV_HEADS * V_AUG, s), BF16),
        jax.ShapeDtypeStruct((b, 512, s), BF16),
    )
    out_specs = (row_blk(RM_COLS), row_blk(LANES), col_blk(16), col_blk(256), row_blk(256),
                 col_blk(N_HEADS * V_AUG), col_blk(256), row_blk(128), col_blk(GQA_KV_HEADS * V_AUG),
                 col_blk(512))
    in_specs = [
        row_blk(d),
        _const_spec((1, d)),
        _const_spec(w_rm.shape), _const_spec(w_gate.shape), _const_spec(w_t.shape),
        _const_spec((1, RM_COLS)), _const_spec((1, RM_COLS)),
        pl.BlockSpec((6, 16, tm), lambda bi, i: (0, 0, i)),
        _const_spec((DIFF_QK_DIM, tm)), _const_spec((DIFF_QK_DIM, tm)),
        _const_spec((HEAD_DIM, tm)), _const_spec((HEAD_DIM, tm)),
    ]
    return pl.pallas_call(
        _inproj_kernel, grid=grid, in_specs=in_specs, out_specs=out_specs, out_shape=out_shapes,
        compiler_params=_params(2), name="in_projection",
    )(x, g[None].astype(F32), w_rm, w_gate, w_t, cscale, cadd, tabs, bc(gqd), bc(gkd), bc(gqg), bc(gkg))


def _attn_kernel(*refs, combos, width, tq, tk, n_extra, finalize, bounded):
    shift_ref, qt_ref, k_ref, vt_ref = refs[:4]
    extra = refs[4:4 + n_extra]
    o_ref = refs[4 + n_extra]
    qpad_sc, m_sc, acc_sc, p_sc = refs[5 + n_extra:]
    n_kv = k_ref.shape[1] // tk
    n_c = len(combos)

    for c, (q_row0, q_rows, k_off, _) in enumerate(combos):
        pieces = []
        if k_off:
            pieces.append(jnp.zeros((k_off, tq), BF16))
        pieces.append(qt_ref[0, q_row0:q_row0 + q_rows, :])
        if width - k_off - q_rows:
            pieces.append(jnp.zeros((width - k_off - q_rows, tq), BF16))
        qpad_sc[c] = jnp.concatenate(pieces, axis=0) if len(pieces) > 1 else pieces[0]
    m_sc[...] = jnp.full(m_sc.shape, NEG_BIG, F32)
    acc_sc[...] = jnp.zeros(acc_sc.shape, F32)
    shift = shift_ref[0, 0]

    def scores(c, start):
        return _dot(k_ref[0, pl.ds(start, tk), :], qpad_sc[c])

    def values(c, start):
        v_row0 = combos[c][3]
        return vt_ref[0, v_row0:v_row0 + V_AUG, pl.ds(start, tk)]

    def qk_exp(c, start, slot):
        p_sc[slot, c] = jnp.exp2(scores(c, start) - shift).astype(BF16)

    def pv(c, start, slot):
        acc_sc[c] += _dot(values(c, start), p_sc[slot, c])

    def bounded_pair(i, carry):
        s0 = pl.multiple_of(2 * i * tk, tk)
        s1 = pl.multiple_of((2 * i + 1) * tk, tk)
        s2 = pl.multiple_of((2 * i + 2) * tk, tk)
        for c in range(n_c):
            qk_exp(c, s1, 1)
            pv(c, s0, 0)
        for c in range(n_c):
            qk_exp(c, s2, 0)
            pv(c, s1, 1)
        return carry

    def online_body(j, carry):
        start = pl.multiple_of(j * tk, tk)
        for c in range(n_c):
            st = scores(c, start)
            m_prev = m_sc[c]
            m_new = jnp.maximum(m_prev, jnp.max(st, axis=0, keepdims=True))
            p = jnp.exp2(st - m_new).astype(BF16)
            acc_sc[c] = jnp.exp2(m_prev - m_new) * acc_sc[c] + _dot(values(c, start), p)
            m_sc[c] = m_new
        return carry

    if bounded:
        for c in range(n_c):
            qk_exp(c, 0, 0)
        lax.fori_loop(0, n_kv // 2 - 1, bounded_pair, 0)
        for c in range(n_c):
            qk_exp(c, (n_kv - 1) * tk, 1)
            pv(c, (n_kv - 2) * tk, 0)
        for c in range(n_c):
            pv(c, (n_kv - 1) * tk, 1)
    else:
        lax.fori_loop(0, n_kv, online_body, 0)
    out_t = finalize([acc_sc[c] for c in range(len(combos))], extra)
    o_ref[0] = out_t.T.astype(BF16)


def _finalize_diff(accs, extra, *, lam_init):
    lam_ref, gsub_ref = extra
    lp = lam_ref[...]
    lam = (jnp.exp(jnp.sum(lp[0:1] * lp[1:2], axis=1, keepdims=True))
           - jnp.exp(jnp.sum(lp[2:3] * lp[3:4], axis=1, keepdims=True)) + lam_init)
    outs = []
    for h in range(N_HEADS):
        a1, a2 = accs[2 * h], accs[2 * h + 1]
        o = (a1[:HEAD_DIM] / a1[HEAD_DIM:HEAD_DIM + 1]
             - lam * (a2[:HEAD_DIM] / a2[HEAD_DIM:HEAD_DIM + 1]))
        ms = jnp.mean(o * o, axis=0, keepdims=True)
        outs.append(o * lax.rsqrt(ms + EPS) * gsub_ref[...] * (1.0 - lam_init))
    return jnp.concatenate(outs, axis=0)


def _finalize_gqa(accs, extra):
    return jnp.concatenate([a[:HEAD_DIM] / a[HEAD_DIM:HEAD_DIM + 1] for a in accs], axis=0)


def _score_bound(g_q, g_k, dim):
    return 1.02 * (dim ** 0.5) * LOG2E * jnp.max(jnp.abs(g_q)) * jnp.max(jnp.abs(g_k))


def _attention(qt, k, vt, extra, bound, *, combos, finalize, tq=256, tk=512):
    b, _, s = qt.shape
    width = k.shape[2]
    n_c = len(combos)
    in_specs = [
        pl.BlockSpec(memory_space=pltpu.SMEM),
        pl.BlockSpec((1, qt.shape[1], tq), lambda bi, i: (bi, 0, i)),
        pl.BlockSpec((1, s, width), lambda bi, i: (bi, 0, 0)),
        pl.BlockSpec((1, vt.shape[1], s), lambda bi, i: (bi, 0, 0)),
    ] + [_const_spec(e.shape) for e in extra]

    def call(bounded):
        kern = functools.partial(_attn_kernel, combos=combos, width=width, tq=tq, tk=tk,
                                 n_extra=len(extra), finalize=finalize, bounded=bounded)
        return pl.pallas_call(
            kern, grid=(b, s // tq), in_specs=in_specs,
            out_specs=pl.BlockSpec((1, tq, 256), lambda bi, i: (bi, i, 0)),
            out_shape=jax.ShapeDtypeStruct((b, s, 256), BF16),
            scratch_shapes=[pltpu.VMEM((n_c, width, tq), BF16),
                            pltpu.VMEM((n_c, 1, tq), F32),
                            pltpu.VMEM((n_c, V_AUG, tq), F32),
                            pltpu.VMEM((2, n_c, tk, tq) if bounded else (1, 1, 16, LANES), BF16)],
            compiler_params=_params(2),
            name="attention_%d_%s" % (width, "bounded" if bounded else "online"),
        )(bound.reshape(1, 1).astype(F32), qt, k, vt, *extra)

    return lax.cond(bound <= MAX_SCORE_BOUND, lambda: call(True), lambda: call(False))


def _diff_attention(qt, k, vt, lam_params, g_sub, lam_init, bound, tq=512):
    combos = tuple((64 * h + 32 * c, 32, 64 * h + 32 * c, V_AUG * h)
                   for h in range(N_HEADS) for c in range(2))
    gsub_b = jnp.broadcast_to(g_sub.astype(F32)[:, None], (HEAD_DIM, tq))
    fin = functools.partial(_finalize_diff, lam_init=lam_init)
    return _attention(qt, k, vt, (lam_params.astype(F32), gsub_b), bound, combos=combos, finalize=fin, tq=tq)


def _gqa_attention(qt, k, vt, bound, tq=512):
    combos = tuple((64 * qh, 64, 64 * (qh // 2), V_AUG * (qh // 2)) for qh in range(N_HEADS))
    return _attention(qt, k, vt, (), bound, combos=combos, finalize=_finalize_gqa, tq=tq)


def _log_sigmoid(x):
    return jnp.minimum(x, 0.0) - jnp.log(1.0 + jnp.exp(-jnp.abs(x)))


def _mlstm_kernel(qf_ref, kf_ref, vf_ref, ktf_ref, gf_ref, gtf_ref,
                  qb_ref, kb_ref, vb_ref, ktb_ref, gb_ref, gtb_ref,
                  brow_ref, bcol_ref, hf_ref, hb_ref, c_sc, m_sc):
    L = ML_CHUNK

    @pl.when(pl.program_id(1) == 0)
    def _():
        c_sc[...] = jnp.zeros(c_sc.shape, F32)
        m_sc[...] = jnp.zeros(m_sc.shape, F32)

    row = lax.broadcasted_iota(jnp.int32, (L, L), 0)
    col = lax.broadcasted_iota(jnp.int32, (L, L), 1)
    lower = col <= row
    upper = col >= row
    lower_f = lower.astype(F32)
    upper_f = upper.astype(F32)
    lane = lax.broadcasted_iota(jnp.int32, (1, LANES), 1)
    forget_lane = (lane % 8) >= 4
    grow = lax.broadcasted_iota(jnp.int32, (16, 1), 0)
    forget_row = (grow % 8) >= 4
    keep = (lax.broadcasted_iota(jnp.int32, (1, LANES), 1) < HEAD_DIM).astype(F32)

    dirs = (
        (0, qf_ref, kf_ref, vf_ref, ktf_ref, gf_ref, gtf_ref, hf_ref, lower, lower_f, upper_f, L - 1),
        (1, qb_ref, kb_ref, vb_ref, ktb_ref, gb_ref, gtb_ref, hb_ref, upper, upper_f, lower_f, 0),
    )
    for d, q_ref, k_ref, v_ref, kt_ref, g_ref, gt_ref, h_ref, mask, tri_c, tri_r, last in dirs:
        g = g_ref[0] + brow_ref[...]
        gp = jnp.where(forget_lane, _log_sigmoid(g), g)
        gt = gt_ref[0] + bcol_ref[...]
        gtp = jnp.where(forget_row, _log_sigmoid(gt), gt)
        bcol_all = _dot_f32(tri_c, gp)
        brow_all = _dot_f32(gtp, tri_r)
        for h in range(N_HEADS):
            ii, fi = d * 8 + h, d * 8 + 4 + h
            idx = d * N_HEADS + h
            li_col = gp[:, ii:ii + 1]
            b_col = bcol_all[:, fi:fi + 1]
            li_row = gtp[ii:ii + 1, :]
            b_row = brow_all[fi:fi + 1, :]
            g_tot = b_row[:, last:last + 1]
            m_prev = m_sc[idx][0:1, 0:1]

            qh = q_ref[0, :, h * LANES:(h + 1) * LANES]
            kh = k_ref[0, :, h * LANES:(h + 1) * LANES]
            vh = v_ref[0, :, h * LANES:(h + 1) * LANES]
            kth = kt_ref[0, h * LANES:(h + 1) * LANES, :]

            dlog = jnp.where(mask, b_col - b_row + li_row, -jnp.inf)
            e_col = b_col + m_prev
            m_t = jnp.maximum(e_col, jnp.max(dlog, axis=1, keepdims=True))
            wts = (jnp.exp(dlog - m_t) * _dot_nt(qh, kh)).astype(BF16)
            s_inter = jnp.exp(e_col - m_t)
            c_prev = c_sc[idx]
            tot = _dot(wts, vh) + s_inter * _dot(qh, c_prev.astype(BF16))
            den = tot[:, HEAD_DIM:HEAD_DIM + 1]
            hout = tot / jnp.maximum(jnp.abs(den), jnp.exp(-m_t))
            h_ref[0, :, h * LANES:(h + 1) * LANES] = (hout * keep).astype(BF16)

            a_row = g_tot - b_row + li_row
            m_loc = jnp.max(a_row, axis=1, keepdims=True)
            w_col = jnp.exp(g_tot - b_col + li_col - m_loc)
            c_loc = _dot(kth, (vh.astype(F32) * w_col).astype(BF16))
            m_new = jnp.maximum(g_tot + m_prev, m_loc)
            f_old = jnp.exp(g_tot + m_prev - m_new)
            f_loc = jnp.exp(m_loc - m_new)
            c_sc[idx] = f_old * c_prev + f_loc * c_loc
            m_sc[idx] = jnp.broadcast_to(m_new, (SUBLANES, LANES))


def _mlstm(rm, mkt, gates, gates_t, b_i, b_f):
    b, s, _ = rm.shape
    L = ML_CHUNK
    nc = s // L
    bias = jnp.stack([b_i[0], b_f[0], b_i[1], b_f[1]]).reshape(16).astype(F32)
    bias_row = jnp.concatenate([bias, jnp.zeros((LANES - 16,), F32)])[None]
    bias_col = jnp.broadcast_to(bias[:, None], (16, L))
    fwd = lambda cblk: (lambda bi, i: (bi, i, cblk))
    bwd = lambda cblk: (lambda bi, i: (bi, nc - 1 - i, cblk))
    in_specs = []
    for mk in (fwd, bwd):
        in_specs += [pl.BlockSpec((1, L, 512), mk(RM_Q // 512)),
                     pl.BlockSpec((1, L, 512), mk(RM_K // 512)),
                     pl.BlockSpec((1, L, 512), mk(RM_V // 512)),
                     pl.BlockSpec((1, 512, L), (lambda bi, i: (bi, 0, i)) if mk is fwd
                                  else (lambda bi, i: (bi, 0, nc - 1 - i))),
                     pl.BlockSpec((1, L, LANES), mk(0)),
                     pl.BlockSpec((1, 16, L), (lambda bi, i: (bi, 0, i)) if mk is fwd
                                  else (lambda bi, i: (bi, 0, nc - 1 - i)))]
    in_specs += [_const_spec((1, LANES)), _const_spec((16, L))]
    out_specs = (pl.BlockSpec((1, L, 512), lambda bi, i: (bi, i, 0)),
                 pl.BlockSpec((1, L, 512), lambda bi, i: (bi, nc - 1 - i, 0)))
    out_shape = (jax.ShapeDtypeStruct((b, s, 512), BF16), jax.ShapeDtypeStruct((b, s, 512), BF16))
    return pl.pallas_call(
        _mlstm_kernel, grid=(b, nc), in_specs=in_specs, out_specs=out_specs, out_shape=out_shape,
        scratch_shapes=[pltpu.VMEM((2 * N_HEADS, LANES, LANES), F32),
                        pltpu.VMEM((2 * N_HEADS, SUBLANES, LANES), F32)],
        compiler_params=_params(2), name="mlstm",
    )(rm, rm, rm, mkt, gates, gates_t, rm, rm, rm, mkt, gates, gates_t, bias_row, bias_col)


def _outproj_kernel(hf_ref, hb_ref, o_ref, pm_ref, pp_ref, pn_ref, yc_ref, yd_ref, x_ref,
                    gml_ref, wpool_ref, pscale_ref, wa_ref, wb_ref, wc_ref, wd_ref, out_ref, *, tm, seq):
    i = pl.program_id(1)
    n_t = pl.num_programs(1)

    hs = hf_ref[0].astype(F32) + hb_ref[0].astype(F32)
    og = o_ref[0].astype(F32)
    parts = []
    for h in range(N_HEADS):
        sl = slice(h * LANES, (h + 1) * LANES)
        hh = hs[:, sl]
        ms = jnp.sum(hh * hh, axis=-1, keepdims=True) * (1.0 / HEAD_DIM)
        gate = 1.0 / (1.0 + jnp.exp(-og[:, sl]))
        parts.append(hh * lax.rsqrt(ms + EPS) * gml_ref[:, sl] * gate)
    ya = jnp.concatenate(parts, axis=1).astype(BF16)

    prev = jnp.where(i > 0, pp_ref[0].astype(F32), 0.0)
    nxt = jnp.where(i < n_t - 1, pn_ref[0].astype(F32), 0.0)
    u = jnp.concatenate([prev, pm_ref[0].astype(F32), nxt], axis=0)
    n = tm + 2 * SUBLANES
    sh = lambda v, k: pltpu.roll(v, k % n, 0)
    w2 = u + sh(u, 1)
    w4 = sh(w2, 1) + sh(w2, -1)
    w8 = sh(w4, 2) + sh(w4, -2)
    w16 = sh(w8, 4) + sh(w8, -4)
    core = slice(SUBLANES, SUBLANES + tm)
    lane = lax.broadcasted_iota(jnp.int32, (1, GROUP_WIDTH), 1)
    t = (i * tm + lax.broadcasted_iota(jnp.int32, (tm, 1), 0)).astype(F32)
    total = jnp.where(lane < 64, w2[core], jnp.where(lane < 128, w4[core],
                      jnp.where(lane < 192, w8[core], w16[core])))
    half = jnp.where(lane < 64, 1.0, jnp.where(lane < 128, 2.0, jnp.where(lane < 192, 4.0, 8.0)))
    lo = jnp.maximum(t - half, 0.0)
    hi = jnp.minimum(t + half - 1.0, seq - 1.0)
    pooled = total / (hi - lo + 1.0) - u[core]
    yb = (_dot(pooled.astype(BF16), wpool_ref[...]) * pscale_ref[...]).astype(BF16)

    out_ref[0] = (x_ref[0] + _dot(ya, wa_ref[...]) + _dot(yb, wb_ref[...])
                  + _dot(yc_ref[0], wc_ref[...]) + _dot(yd_ref[0], wd_ref[...]))


def _out_projection(hf, hb, rm, yc, yd, x, ml_norm, pool_w, pool_scale, w_out, tm=512):
    b, s, d = x.shape
    wa = w_out[0:256].reshape(N_HEADS, HEAD_DIM, d)
    wa = jnp.concatenate([wa, jnp.zeros_like(wa)], axis=1).reshape(N_HEADS * LANES, d).astype(BF16)
    wb, wc, wd = (w_out[256 * j:256 * (j + 1)].astype(BF16) for j in (1, 2, 3))
    gml = ml_norm.reshape(N_HEADS, HEAD_DIM)
    gml = jnp.concatenate([gml, jnp.zeros_like(gml)], axis=1).reshape(1, N_HEADS * LANES).astype(F32)
    wpool = jax.scipy.linalg.block_diag(*[pool_w[g] for g in range(4)]).astype(BF16)
    nb8 = tm // SUBLANES
    last8 = s // SUBLANES - 1
    row = lambda w, cb=0: pl.BlockSpec((1, tm, w), lambda bi, i: (bi, i, cb))
    in_specs = [
        row(512), row(512), row(512, RM_O // 512),
        row(256, RM_POOL // 256),
        pl.BlockSpec((1, SUBLANES, 256), lambda bi, i: (bi, jnp.maximum(i * nb8 - 1, 0), RM_POOL // 256)),
        pl.BlockSpec((1, SUBLANES, 256), lambda bi, i: (bi, jnp.minimum((i + 1) * nb8, last8), RM_POOL // 256)),
        row(256), row(256), row(d),
        _const_spec((1, 512)), _const_spec((256, 256)), _const_spec((1, 256)),
        _const_spec((512, d)), _const_spec((256, d)), _const_spec((256, d)), _const_spec((256, d)),
    ]
    kern = functools.partial(_outproj_kernel, tm=tm, seq=s)
    return pl.pallas_call(
        kern, grid=(b, s // tm), in_specs=in_specs, out_specs=row(d),
        out_shape=jax.ShapeDtypeStruct((b, s, d), F32),
        compiler_params=_params(2), name="out_projection",
    )(hf, hb, rm, rm, rm, rm, yc, yd, x, gml, wpool, pool_scale[None].astype(F32), wa, wb, wc, wd)


def _memkv_kernel(mem_ref, g_ref, w_ref, gk_ref, k_ref, v_ref):
    mn = _rms(mem_ref[0], g_ref[...]).astype(BF16)
    kv = _dot(mn, w_ref[...])
    ks = []
    for h in range(CROSS_HEADS):
        ks.append(_rms(kv[:, h * CROSS_HEAD_DIM:(h + 1) * CROSS_HEAD_DIM], gk_ref[...]))
    k_ref[0] = jnp.concatenate(ks, axis=1).astype(BF16)
    v_ref[0] = kv[:, D_MODEL:].astype(BF16)


def _mem_kv(mem, g_mem, w_kv, g_k):
    b, m, d = mem.shape
    blk = pl.BlockSpec((1, m, d), lambda bi: (bi, 0, 0))
    return pl.pallas_call(
        _memkv_kernel, grid=(b,),
        in_specs=[blk, _const_spec((1, d)), _const_spec((d, 2 * d)), _const_spec((1, CROSS_HEAD_DIM))],
        out_specs=(blk, blk),
        out_shape=(jax.ShapeDtypeStruct((b, m, d), BF16), jax.ShapeDtypeStruct((b, m, d), BF16)),
        compiler_params=_params(1), name="mem_kv",
    )(mem, g_mem[None].astype(F32), w_kv.astype(BF16), g_k[None].astype(F32))


def _cross_kernel(x_ref, g_ref, wq_ref, gq_ref, k_ref, v_ref, wo_ref, out_ref):
    x = x_ref[0]
    xn = _rms(x, g_ref[...]).astype(BF16)
    q = _dot(xn, wq_ref[...])
    outs = []
    for h in range(CROSS_HEADS):
        sl = slice(h * CROSS_HEAD_DIM, (h + 1) * CROSS_HEAD_DIM)
        qh = (_rms(q[:, sl], gq_ref[...]) * (CROSS_HEAD_DIM ** -0.5)).astype(BF16)
        s = _dot_nt(qh, k_ref[0, :, sl])
        p = jnp.exp(s - jnp.max(s, axis=-1, keepdims=True))
        l = jnp.sum(p, axis=-1, keepdims=True)
        outs.append((_dot(p.astype(BF16), v_ref[0, :, sl]) / l).astype(BF16))
    o = jnp.concatenate(outs, axis=1)
    out_ref[0] = x + _dot(o, wo_ref[...])


def _cross_attention(x, g, w_q, g_q, k, v, w_o, tm=512):
    b, s, d = x.shape
    m = k.shape[1]
    row = pl.BlockSpec((1, tm, d), lambda bi, i: (bi, i, 0))
    kvb = pl.BlockSpec((1, m, d), lambda bi, i: (bi, 0, 0))
    return pl.pallas_call(
        _cross_kernel, grid=(b, s // tm),
        in_specs=[row, _const_spec((1, d)), _const_spec((d, d)), _const_spec((1, CROSS_HEAD_DIM)),
                  kvb, kvb, _const_spec((d, d))],
        out_specs=row, out_shape=jax.ShapeDtypeStruct((b, s, d), F32),
        compiler_params=_params(2), name="cross_attention",
    )(x, g[None].astype(F32), w_q.astype(BF16), g_q[None].astype(F32), k, v, w_o.astype(BF16))


FF_CHUNK = 256


def _ffn_kernel(xm_ref, xp_ref, xn_ref, g_ref, win_ref, cw_ref, cb_ref, wout_ref, out_ref, *, tm):
    i = pl.program_id(1)
    n_t = pl.num_programs(1)
    xm = xm_ref[0]
    xe = jnp.concatenate([xp_ref[0], xm, xn_ref[0]], axis=0)
    hn = _rms(xe, g_ref[...]).astype(BF16)
    n = tm + 2 * SUBLANES
    core = slice(SUBLANES, SUBLANES + tm)
    hc = hn[core]
    r = lax.broadcasted_iota(jnp.int32, (n, 1), 0)
    live = jnp.logical_and(jnp.logical_or(i > 0, r >= SUBLANES),
                           jnp.logical_or(i < n_t - 1, r < SUBLANES + tm)).astype(F32)
    acc = xm
    for c in range(D_FF // FF_CHUNK):
        gs = slice(c * FF_CHUNK, (c + 1) * FF_CHUNK)
        us = slice(D_FF + c * FF_CHUNK, D_FF + (c + 1) * FF_CHUNK)
        gate = _dot(hn, win_ref[:, gs]) * live
        cw = cw_ref[:, gs]
        gconv = (pltpu.roll(gate, 1, 0)[core] * cw[0:1] + gate[core] * cw[1:2]
                 + pltpu.roll(gate, n - 1, 0)[core] * cw[2:3] + cb_ref[:, gs])
        up = _dot(hc, win_ref[:, us])
        act = (gconv / (1.0 + jnp.exp(-gconv)) * up).astype(BF16)
        acc = acc + _dot(act, wout_ref[gs, :])
    out_ref[0] = acc


def _conv_ffn(x, g, w_in, conv_w, conv_b, w_out, tm=512):
    b, s, d = x.shape
    nb8 = tm // SUBLANES
    last8 = s // SUBLANES - 1
    row = pl.BlockSpec((1, tm, d), lambda bi, i: (bi, i, 0))
    in_specs = [
        row,
        pl.BlockSpec((1, SUBLANES, d), lambda bi, i: (bi, jnp.maximum(i * nb8 - 1, 0), 0)),
        pl.BlockSpec((1, SUBLANES, d), lambda bi, i: (bi, jnp.minimum((i + 1) * nb8, last8), 0)),
        _const_spec((1, d)), _const_spec((d, 2 * D_FF)), _const_spec((3, D_FF)), _const_spec((1, D_FF)),
        _const_spec((D_FF, d)),
    ]
    return pl.pallas_call(
        functools.partial(_ffn_kernel, tm=tm), grid=(b, s // tm), in_specs=in_specs, out_specs=row,
        out_shape=jax.ShapeDtypeStruct((b, s, d), F32),
        compiler_params=_params(2), name="conv_ffn",
    )(x, x, x, g[None].astype(F32), w_in.astype(BF16), conv_w.astype(F32), conv_b[None].astype(F32),
      w_out.astype(BF16))


def _rope_tables(s):
    pos = jnp.arange(s)
    row = pos // GRID_W
    col = pos - row * GRID_W
    inv = ROPE_THETA ** (-jnp.arange(0, DIFF_QK_DIM, 2, dtype=F32) / DIFF_QK_DIM)
    out = []
    for p in (pos, row, col):
        ang = inv[:, None] * p.astype(F32)[None, :]
        out += [jnp.cos(ang), jnp.sin(ang)]
    return jnp.stack(out)


def kernel(x, mem, norm_mix, w_in, ml_bias_i, ml_bias_f, ml_norm, pool_w, pool_scale, diff_qnorm, diff_knorm, diff_lambda, diff_subnorm, gqa_qnorm, gqa_knorm, w_out, norm_cross, norm_mem, w_cq, w_ckv, cross_qnorm, cross_knorm, w_co, norm_ffn, w_ffn_in, ffn_conv, ffn_conv_b, w_ffn_out):
    depth = w_in.shape[0]
    tabs = _rope_tables(x.shape[1])
    for l in range(depth):
        lam_init = 0.8 - 0.6 * math.exp(-0.3 * l)
        w_rm, w_gate, w_t = _inproj_weights(w_in[l])
        (rm, gates, gates_t, qtd, kd, vtd, qtg, kg, vtg, mkt) = _in_projection(
            x, norm_mix[l], w_rm, w_gate, w_t, tabs, diff_qnorm[l], diff_knorm[l], gqa_qnorm[l], gqa_knorm[l])
        hf, hb = _mlstm(rm, mkt, gates, gates_t, ml_bias_i[l], ml_bias_f[l])
        yc = _diff_attention(qtd, kd, vtd, diff_lambda[l], diff_subnorm[l], lam_init,
                             _score_bound(diff_qnorm[l], diff_knorm[l], DIFF_QK_DIM))
        yd = _gqa_attention(qtg, kg, vtg, _score_bound(gqa_qnorm[l], gqa_knorm[l], HEAD_DIM))
        x = _out_projection(hf, hb, rm, yc, yd, x, ml_norm[l], pool_w[l], pool_scale[l], w_out[l])
        ck, cv = _mem_kv(mem, norm_mem[l], w_ckv[l], cross_knorm[l])
        x = _cross_attention(x, norm_cross[l], w_cq[l], cross_qnorm[l], ck, cv, w_co[l])
        x = _conv_ffn(x, norm_ffn[l], w_ffn_in[l], ffn_conv[l], ffn_conv_b[l], w_ffn_out[l])
    return x
```

```python
import functools
import math

import jax
import jax.numpy as jnp
from jax import lax
from jax.experimental import pallas as pl
from jax.experimental.pallas import tpu as pltpu

F32 = jnp.float32
BF16 = jnp.bfloat16

D_MODEL = 1024
HEAD_DIM = 64
GROUP_WIDTH = 256
N_HEADS = 4
POOL_WINDOWS = (2, 4, 8, 16)
DIFF_QK_DIM = 32
GQA_KV_HEADS = 2
GRID_W = 64
ROPE_THETA = 10000.0
CROSS_HEADS = 4
CROSS_HEAD_DIM = 256
D_FF = 2816
EPS = 1e-6

LANES = 128
SUBLANES = 8
VMEM_LIMIT_BYTES = 56 * 1024 * 1024

V_AUG = HEAD_DIM + 16
ML_CHUNK = 128
NEG_BIG = -1e30
LOG2E = math.log2(math.e)
MAX_SCORE_BOUND = 50.0

_NT = (((1,), (1,)), ((), ()))


def _dot(a, b):
    return jnp.dot(a, b, preferred_element_type=F32)


def _dot_nt(a, b):
    return lax.dot_general(a, b, _NT, preferred_element_type=F32)


def _dot_f32(a, b):
    return jnp.dot(a, b, preferred_element_type=F32, precision=lax.Precision.HIGHEST)


def _params(n_axes):
    return pltpu.CompilerParams(dimension_semantics=("arbitrary",) * n_axes,
                                vmem_limit_bytes=VMEM_LIMIT_BYTES)


def _const_spec(shape):
    nd = len(shape)
    return pl.BlockSpec(shape, lambda *_: (0,) * nd, pipeline_mode=pl.Buffered(1))


def _rms(x, g):
    ms = jnp.mean(x * x, axis=-1, keepdims=True)
    return x * lax.rsqrt(ms + EPS) * g


RM_Q, RM_K, RM_V, RM_O, RM_POOL = 0, 512, 1024, 1536, 2048
RM_COLS = 2304
T_DQ, T_DK, T_GQ, T_GK, T_DV, T_GV, T_MK, T_GATE = 0, 256, 512, 768, 896, 1152, 1280, 1792
T_ROWS = 1808


def _norm_rope_t(z, norm_rows, gain, cos64, sin64, scale):
    r, t = z.shape
    zg = z.reshape(r // norm_rows, norm_rows, t)
    ms = jnp.mean(zg * zg, axis=1, keepdims=True)
    zn = (zg * lax.rsqrt(ms + EPS) * gain[None]).reshape(r // 64, 64, t)
    sw = jnp.concatenate([zn[:, 16:32], zn[:, 0:16], zn[:, 48:64], zn[:, 32:48]], axis=1)
    out = zn * cos64[None] + sw * sin64[None]
    return (out * scale).reshape(r, t)


def _with_ones_rows(vt, heads):
    t = vt.shape[1]
    v3 = vt.reshape(heads, HEAD_DIM, t)
    ones = jnp.ones((heads, V_AUG - HEAD_DIM, t), F32)
    return jnp.concatenate([v3, ones], axis=1).reshape(heads * V_AUG, t)


def _inproj_kernel(x_ref, g_ref, wrm_ref, wg_ref, wt_ref, cscale_ref, cadd_ref, tabs_ref,
                   gqd_ref, gkd_ref, gqg_ref, gkg_ref,
                   rm_ref, gates_ref, gates_t_ref, qtd_ref, kd_ref, vtd_ref,
                   qtg_ref, kg_ref, vtg_ref, mkt_ref):
    x = x_ref[0]
    h = _rms(x, g_ref[...]).astype(BF16)
    zr = _dot(h, wrm_ref[...])
    rm_ref[0] = (zr * cscale_ref[...] + cadd_ref[...]).astype(BF16)
    gates_ref[0] = _dot(h, wg_ref[...])
    zt = _dot_nt(wt_ref[...], h)

    tabs = tabs_ref[...]
    c1, s1, rc, rs, cc, cs = (tabs[i] for i in range(6))
    cos_d = jnp.concatenate([c1, c1, c1, c1], axis=0)
    sin_d = jnp.concatenate([-s1, s1, -s1, s1], axis=0)
    cos_g = jnp.concatenate([rc, rc, cc, cc], axis=0)
    sin_g = jnp.concatenate([-rs, rs, -cs, cs], axis=0)

    qd = _norm_rope_t(zt[T_DQ:T_DQ + 256], DIFF_QK_DIM, gqd_ref[...], cos_d, sin_d,
                      DIFF_QK_DIM ** -0.5 * LOG2E)
    qtd_ref[0] = qd.astype(BF16)
    kd = _norm_rope_t(zt[T_DK:T_DK + 256], DIFF_QK_DIM, gkd_ref[...], cos_d, sin_d, 1.0)
    kd_ref[0] = kd.T.astype(BF16)
    qg = _norm_rope_t(zt[T_GQ:T_GQ + 256], HEAD_DIM, gqg_ref[...], cos_g, sin_g, HEAD_DIM ** -0.5 * LOG2E)
    qtg_ref[0] = qg.astype(BF16)
    kg = _norm_rope_t(zt[T_GK:T_GK + 128], HEAD_DIM, gkg_ref[...], cos_g, sin_g, 1.0)
    kg_ref[0] = kg.T.astype(BF16)
    vtd_ref[0] = _with_ones_rows(zt[T_DV:T_DV + 256], N_HEADS).astype(BF16)
    vtg_ref[0] = _with_ones_rows(zt[T_GV:T_GV + 128], GQA_KV_HEADS).astype(BF16)
    mkt_ref[0] = (zt[T_MK:T_MK + 512] * (HEAD_DIM ** -0.5)).astype(BF16)
    gates_t_ref[0] = zt[T_GATE:T_GATE + 16]


def _pad_heads_cols(w):
    d = w.shape[0]
    w4 = w.reshape(d, N_HEADS, HEAD_DIM)
    return jnp.concatenate([w4, jnp.zeros_like(w4)], axis=2).reshape(d, N_HEADS * LANES)


def _inproj_weights(w_in):
    sizes = (256, 256, 256, 256, 16, 256, 256, 256, 256, 256, 128, 128)
    offs = [0]
    for s in sizes:
        offs.append(offs[-1] + s)
    (ml_q, ml_k, ml_v, ml_o, ml_g, pool, d_q, d_k, d_v, g_q, g_k, g_v) = (
        w_in[:, offs[i]:offs[i + 1]] for i in range(12))
    w_rm = jnp.concatenate([_pad_heads_cols(ml_q), _pad_heads_cols(ml_k), _pad_heads_cols(ml_v),
                            _pad_heads_cols(ml_o), pool], axis=1).astype(BF16)
    w_gate = jnp.concatenate([ml_g, jnp.zeros((D_MODEL, LANES - 16), F32)], axis=1).astype(BF16)
    w_t = jnp.concatenate([d_q, d_k, g_q, g_k, d_v, g_v, _pad_heads_cols(ml_k), ml_g], axis=1).T.astype(BF16)
    return w_rm, w_gate, w_t


def _inproj_col_consts():
    lane = jnp.arange(RM_COLS)
    in_k = (lane >= RM_K) & (lane < RM_V)
    cscale = jnp.where(in_k, HEAD_DIM ** -0.5, 1.0).astype(F32)[None]
    ones_col = (lane >= RM_V) & (lane < RM_O) & ((lane % LANES) == HEAD_DIM)
    cadd = jnp.where(ones_col, 1.0, 0.0).astype(F32)[None]
    return cscale, cadd


def _in_projection(x, g, w_rm, w_gate, w_t, tabs, gqd, gkd, gqg, gkg, tm=512):
    b, s, d = x.shape
    cscale, cadd = _inproj_col_consts()
    bc = lambda v: jnp.broadcast_to(v.astype(F32)[:, None], (v.shape[0], tm))
    grid = (b, s // tm)
    row_blk = lambda w: pl.BlockSpec((1, tm, w), lambda bi, i: (bi, i, 0))
    col_blk = lambda r: pl.BlockSpec((1, r, tm), lambda bi, i: (bi, 0, i))
    out_shapes = (
        jax.ShapeDtypeStruct((b, s, RM_COLS), BF16),
        jax.ShapeDtypeStruct((b, s, LANES), F32),
        jax.ShapeDtypeStruct((b, 16, s), F32),
        jax.ShapeDtypeStruct((b, 256, s), BF16),
        jax.ShapeDtypeStruct((b, s, 256), BF16),
        jax.ShapeDtypeStruct((b, N_HEADS * V_AUG, s), BF16),
        jax.ShapeDtypeStruct((b, 256, s), BF16),
        jax.ShapeDtypeStruct((b, s, 128), BF16),
        jax.ShapeDtypeStruct((b, GQA_KV_HEADS * V_AUG, s), BF16),
        jax.ShapeDtypeStruct((b, 512, s), BF16),
    )
    out_specs = (row_blk(RM_COLS), row_blk(LANES), col_blk(16), col_blk(256), row_blk(256),
                 col_blk(N_HEADS * V_AUG), col_blk(256), row_blk(128), col_blk(GQA_KV_HEADS * V_AUG),
                 col_blk(512))
    in_specs = [
        row_blk(d),
        _const_spec((1, d)),
        _const_spec(w_rm.shape), _const_spec(w_gate.shape), _const_spec(w_t.shape),
        _const_spec((1, RM_COLS)), _const_spec((1, RM_COLS)),
        pl.BlockSpec((6, 16, tm), lambda bi, i: (0, 0, i)),
        _const_spec((DIFF_QK_DIM, tm)), _const_spec((DIFF_QK_DIM, tm)),
        _const_spec((HEAD_DIM, tm)), _const_spec((HEAD_DIM, tm)),
    ]
    return pl.pallas_call(
        _inproj_kernel, grid=grid, in_specs=in_specs, out_specs=out_specs, out_shape=out_shapes,
        compiler_params=_params(2), name="in_projection",
    )(x, g[None].astype(F32), w_rm, w_gate, w_t, cscale, cadd, tabs, bc(gqd), bc(gkd), bc(gqg), bc(gkg))


def _attn_kernel(*refs, combos, width, tq, tk, n_extra, finalize, bounded):
    qt_ref, k_ref, vt_ref = refs[:3]
    extra = refs[3:3 + n_extra]
    o_ref = refs[3 + n_extra]
    qpad_sc, m_sc, acc_sc, p_sc = refs[4 + n_extra:]
    n_kv = k_ref.shape[1] // tk
    n_c = len(combos)

    for c, (q_row0, q_rows, k_off, _) in enumerate(combos):
        pieces = []
        if k_off:
            pieces.append(jnp.zeros((k_off, tq), BF16))
        pieces.append(qt_ref[0, q_row0:q_row0 + q_rows, :])
        if width - k_off - q_rows:
            pieces.append(jnp.zeros((width - k_off - q_rows, tq), BF16))
        qpad_sc[c] = jnp.concatenate(pieces, axis=0) if len(pieces) > 1 else pieces[0]
    m_sc[...] = jnp.full(m_sc.shape, NEG_BIG, F32)
    acc_sc[...] = jnp.zeros(acc_sc.shape, F32)

    def scores(c, start):
        return _dot(k_ref[0, pl.ds(start, tk), :], qpad_sc[c])

    def values(c, start):
        v_row0 = combos[c][3]
        return vt_ref[0, v_row0:v_row0 + V_AUG, pl.ds(start, tk)]

    def qk_exp(c, start, slot):
        p_sc[slot, c] = jnp.exp2(scores(c, start).astype(BF16))

    def pv(c, start, slot):
        acc_sc[c] += _dot(values(c, start), p_sc[slot, c])

    def bounded_pair(i, carry):
        s0 = pl.multiple_of(2 * i * tk, tk)
        s1 = pl.multiple_of((2 * i + 1) * tk, tk)
        s2 = pl.multiple_of((2 * i + 2) * tk, tk)
        for c in range(n_c):
            qk_exp(c, s1, 1)
            pv(c, s0, 0)
        for c in range(n_c):
            qk_exp(c, s2, 0)
            pv(c, s1, 1)
        return carry

    def online_body(j, carry):
        start = pl.multiple_of(j * tk, tk)
        for c in range(n_c):
            st = scores(c, start)
            m_prev = m_sc[c]
            m_new = jnp.maximum(m_prev, jnp.max(st, axis=0, keepdims=True))
            p = jnp.exp2(st - m_new).astype(BF16)
            acc_sc[c] = jnp.exp2(m_prev - m_new) * acc_sc[c] + _dot(values(c, start), p)
            m_sc[c] = m_new
        return carry

    if bounded:
        for c in range(n_c):
            qk_exp(c, 0, 0)
        lax.fori_loop(0, n_kv // 2 - 1, bounded_pair, 0)
        for c in range(n_c):
            qk_exp(c, (n_kv - 1) * tk, 1)
            pv(c, (n_kv - 2) * tk, 0)
        for c in range(n_c):
            pv(c, (n_kv - 1) * tk, 1)
    else:
        lax.fori_loop(0, n_kv, online_body, 0)
    out_t = finalize([acc_sc[c] for c in range(len(combos))], extra)
    o_ref[0] = out_t.T.astype(BF16)


def _finalize_diff(accs, extra, *, lam_init):
    lam_ref, gsub_ref = extra
    lp = lam_ref[...]
    lam = (jnp.exp(jnp.sum(lp[0:1] * lp[1:2], axis=1, keepdims=True))
           - jnp.exp(jnp.sum(lp[2:3] * lp[3:4], axis=1, keepdims=True)) + lam_init)
    outs = []
    for h in range(N_HEADS):
        a1, a2 = accs[2 * h], accs[2 * h + 1]
        o = (a1[:HEAD_DIM] / a1[HEAD_DIM:HEAD_DIM + 1]
             - lam * (a2[:HEAD_DIM] / a2[HEAD_DIM:HEAD_DIM + 1]))
        ms = jnp.mean(o * o, axis=0, keepdims=True)
        outs.append(o * lax.rsqrt(ms + EPS) * gsub_ref[...] * (1.0 - lam_init))
    return jnp.concatenate(outs, axis=0)


def _finalize_gqa(accs, extra):
    return jnp.concatenate([a[:HEAD_DIM] / a[HEAD_DIM:HEAD_DIM + 1] for a in accs], axis=0)


def _score_bound(g_q, g_k, dim):
    return 1.02 * (dim ** 0.5) * LOG2E * jnp.max(jnp.abs(g_q)) * jnp.max(jnp.abs(g_k))


def _attention(qt, k, vt, extra, bound, *, combos, finalize, tq=256, tk=512):
    b, _, s = qt.shape
    width = k.shape[2]
    n_c = len(combos)
    in_specs = [
        pl.BlockSpec((1, qt.shape[1], tq), lambda bi, i: (bi, 0, i)),
        pl.BlockSpec((1, s, width), lambda bi, i: (bi, 0, 0)),
        pl.BlockSpec((1, vt.shape[1], s), lambda bi, i: (bi, 0, 0)),
    ] + [_const_spec(e.shape) for e in extra]

    def call(bounded):
        kern = functools.partial(_attn_kernel, combos=combos, width=width, tq=tq, tk=tk,
                                 n_extra=len(extra), finalize=finalize, bounded=bounded)
        return pl.pallas_call(
            kern, grid=(b, s // tq), in_specs=in_specs,
            out_specs=pl.BlockSpec((1, tq, 256), lambda bi, i: (bi, i, 0)),
            out_shape=jax.ShapeDtypeStruct((b, s, 256), BF16),
            scratch_shapes=[pltpu.VMEM((n_c, width, tq), BF16),
                            pltpu.VMEM((n_c, 1, tq), F32),
                            pltpu.VMEM((n_c, V_AUG, tq), F32),
                            pltpu.VMEM((2, n_c, tk, tq) if bounded else (1, 1, 16, LANES), BF16)],
            compiler_params=_params(2),
            name="attention_%d_%s" % (width, "bounded" if bounded else "online"),
        )(qt, k, vt, *extra)

    return lax.cond(bound <= MAX_SCORE_BOUND, lambda: call(True), lambda: call(False))


def _diff_attention(qt, k, vt, lam_params, g_sub, lam_init, bound, tq=512):
    combos = tuple((64 * h + 32 * c, 32, 64 * h + 32 * c, V_AUG * h)
                   for h in range(N_HEADS) for c in range(2))
    gsub_b = jnp.broadcast_to(g_sub.astype(F32)[:, None], (HEAD_DIM, tq))
    fin = functools.partial(_finalize_diff, lam_init=lam_init)
    return _attention(qt, k, vt, (lam_params.astype(F32), gsub_b), bound, combos=combos, finalize=fin, tq=tq)


def _gqa_attention(qt, k, vt, bound, tq=512):
    combos = tuple((64 * qh, 64, 64 * (qh // 2), V_AUG * (qh // 2)) for qh in range(N_HEADS))
    return _attention(qt, k, vt, (), bound, combos=combos, finalize=_finalize_gqa, tq=tq)


def _log_sigmoid(x):
    return jnp.minimum(x, 0.0) - jnp.log(1.0 + jnp.exp(-jnp.abs(x)))


def _mlstm_kernel(qf_ref, kf_ref, vf_ref, ktf_ref, gf_ref, gtf_ref,
                  qb_ref, kb_ref, vb_ref, ktb_ref, gb_ref, gtb_ref,
                  brow_ref, bcol_ref, hf_ref, hb_ref, c_sc, m_sc):
    L = ML_CHUNK

    @pl.when(pl.program_id(1) == 0)
    def _():
        c_sc[...] = jnp.zeros(c_sc.shape, F32)
        m_sc[...] = jnp.zeros(m_sc.shape, F32)

    row = lax.broadcasted_iota(jnp.int32, (L, L), 0)
    col = lax.broadcasted_iota(jnp.int32, (L, L), 1)
    lower = col <= row
    upper = col >= row
    lower_f = lower.astype(F32)
    upper_f = upper.astype(F32)
    lane = lax.broadcasted_iota(jnp.int32, (1, LANES), 1)
    forget_lane = (lane % 8) >= 4
    grow = lax.broadcasted_iota(jnp.int32, (16, 1), 0)
    forget_row = (grow % 8) >= 4
    keep = (lax.broadcasted_iota(jnp.int32, (1, LANES), 1) < HEAD_DIM).astype(F32)

    dirs = (
        (0, qf_ref, kf_ref, vf_ref, ktf_ref, gf_ref, gtf_ref, hf_ref, lower, lower_f, upper_f, L - 1),
        (1, qb_ref, kb_ref, vb_ref, ktb_ref, gb_ref, gtb_ref, hb_ref, upper, upper_f, lower_f, 0),
    )
    for d, q_ref, k_ref, v_ref, kt_ref, g_ref, gt_ref, h_ref, mask, tri_c, tri_r, last in dirs:
        g = g_ref[0] + brow_ref[...]
        gp = jnp.where(forget_lane, _log_sigmoid(g), g)
        gt = gt_ref[0] + bcol_ref[...]
        gtp = jnp.where(forget_row, _log_sigmoid(gt), gt)
        bcol_all = _dot_f32(tri_c, gp)
        brow_all = _dot_f32(gtp, tri_r)
        for h in range(N_HEADS):
            ii, fi = d * 8 + h, d * 8 + 4 + h
            idx = d * N_HEADS + h
            b_col = bcol_all[:, fi:fi + 1]
            b_row = brow_all[fi:fi + 1, :]
            c_row = gtp[ii:ii + 1, :] - b_row
            g_tot = b_row[:, last:last + 1]
            m_prev = m_sc[idx][0:1, 0:1]

            qh = q_ref[0, :, h * LANES:(h + 1) * LANES]
            kh = k_ref[0, :, h * LANES:(h + 1) * LANES]
            vh = v_ref[0, :, h * LANES:(h + 1) * LANES]
            kth = kt_ref[0, h * LANES:(h + 1) * LANES, :]

            cmask = jnp.where(mask, c_row, -jnp.inf)
            mu = jnp.maximum(m_prev, jnp.max(cmask, axis=1, keepdims=True))
            wts = (jnp.exp(cmask - mu) * _dot_nt(qh, kh)).astype(BF16)
            s_inter = jnp.exp(m_prev - mu)
            c_prev = c_sc[idx]
            tot = _dot(wts, vh) + s_inter * _dot(qh, c_prev.astype(BF16))
            den = tot[:, HEAD_DIM:HEAD_DIM + 1]
            hout = tot / jnp.maximum(jnp.abs(den), jnp.exp(-(b_col + mu)))
            h_ref[0, :, h * LANES:(h + 1) * LANES] = (hout * keep).astype(BF16)

            c_max = jnp.max(c_row, axis=1, keepdims=True)
            m_loc = g_tot + c_max
            w_row = jnp.exp(c_row - c_max)
            c_loc = _dot((kth.astype(F32) * w_row).astype(BF16), vh)
            m_new = jnp.maximum(g_tot + m_prev, m_loc)
            c_sc[idx] = jnp.exp(g_tot + m_prev - m_new) * c_prev + jnp.exp(m_loc - m_new) * c_loc
            m_sc[idx] = jnp.broadcast_to(m_new, (SUBLANES, LANES))


def _mlstm(rm, mkt, gates, gates_t, b_i, b_f):
    b, s, _ = rm.shape
    L = ML_CHUNK
    nc = s // L
    bias = jnp.stack([b_i[0], b_f[0], b_i[1], b_f[1]]).reshape(16).astype(F32)
    bias_row = jnp.concatenate([bias, jnp.zeros((LANES - 16,), F32)])[None]
    bias_col = jnp.broadcast_to(bias[:, None], (16, L))
    fwd = lambda cblk: (lambda bi, i: (bi, i, cblk))
    bwd = lambda cblk: (lambda bi, i: (bi, nc - 1 - i, cblk))
    in_specs = []
    for mk in (fwd, bwd):
        in_specs += [pl.BlockSpec((1, L, 512), mk(RM_Q // 512)),
                     pl.BlockSpec((1, L, 512), mk(RM_K // 512)),
                     pl.BlockSpec((1, L, 512), mk(RM_V // 512)),
                     pl.BlockSpec((1, 512, L), (lambda bi, i: (bi, 0, i)) if mk is fwd
                                  else (lambda bi, i: (bi, 0, nc - 1 - i))),
                     pl.BlockSpec((1, L, LANES), mk(0)),
                     pl.BlockSpec((1, 16, L), (lambda bi, i: (bi, 0, i)) if mk is fwd
                                  else (lambda bi, i: (bi, 0, nc - 1 - i)))]
    in_specs += [_const_spec((1, LANES)), _const_spec((16, L))]
    out_specs = (pl.BlockSpec((1, L, 512), lambda bi, i: (bi, i, 0)),
                 pl.BlockSpec((1, L, 512), lambda bi, i: (bi, nc - 1 - i, 0)))
    out_shape = (jax.ShapeDtypeStruct((b, s, 512), BF16), jax.ShapeDtypeStruct((b, s, 512), BF16))
    return pl.pallas_call(
        _mlstm_kernel, grid=(b, nc), in_specs=in_specs, out_specs=out_specs, out_shape=out_shape,
        scratch_shapes=[pltpu.VMEM((2 * N_HEADS, LANES, LANES), F32),
                        pltpu.VMEM((2 * N_HEADS, SUBLANES, LANES), F32)],
        compiler_params=_params(2), name="mlstm",
    )(rm, rm, rm, mkt, gates, gates_t, rm, rm, rm, mkt, gates, gates_t, bias_row, bias_col)


def _outproj_kernel(hf_ref, hb_ref, o_ref, pm_ref, pp_ref, pn_ref, yc_ref, yd_ref, x_ref,
                    gml_ref, wpool_ref, pscale_ref, wa_ref, wb_ref, wc_ref, wd_ref, out_ref, *, tm, seq):
    i = pl.program_id(1)
    n_t = pl.num_programs(1)

    hs = hf_ref[0].astype(F32) + hb_ref[0].astype(F32)
    og = o_ref[0].astype(F32)
    parts = []
    for h in range(N_HEADS):
        sl = slice(h * LANES, (h + 1) * LANES)
        hh = hs[:, sl]
        ms = jnp.sum(hh * hh, axis=-1, keepdims=True) * (1.0 / HEAD_DIM)
        gate = 1.0 / (1.0 + jnp.exp(-og[:, sl]))
        parts.append(hh * lax.rsqrt(ms + EPS) * gml_ref[:, sl] * gate)
    ya = jnp.concatenate(parts, axis=1).astype(BF16)

    prev = jnp.where(i > 0, pp_ref[0].astype(F32), 0.0)
    nxt = jnp.where(i < n_t - 1, pn_ref[0].astype(F32), 0.0)
    u = jnp.concatenate([prev, pm_ref[0].astype(F32), nxt], axis=0)
    n = tm + 2 * SUBLANES
    sh = lambda v, k: pltpu.roll(v, k % n, 0)
    w2 = u + sh(u, 1)
    w4 = sh(w2, 1) + sh(w2, -1)
    w8 = sh(w4, 2) + sh(w4, -2)
    w16 = sh(w8, 4) + sh(w8, -4)
    core = slice(SUBLANES, SUBLANES + tm)
    lane = lax.broadcasted_iota(jnp.int32, (1, GROUP_WIDTH), 1)
    t = (i * tm + lax.broadcasted_iota(jnp.int32, (tm, 1), 0)).astype(F32)
    total = jnp.where(lane < 64, w2[core], jnp.where(lane < 128, w4[core],
                      jnp.where(lane < 192, w8[core], w16[core])))
    half = jnp.where(lane < 64, 1.0, jnp.where(lane < 128, 2.0, jnp.where(lane < 192, 4.0, 8.0)))
    lo = jnp.maximum(t - half, 0.0)
    hi = jnp.minimum(t + half - 1.0, seq - 1.0)
    pooled = total / (hi - lo + 1.0) - u[core]
    yb = (_dot(pooled.astype(BF16), wpool_ref[...]) * pscale_ref[...]).astype(BF16)

    out_ref[0] = (x_ref[0] + _dot(ya, wa_ref[...]) + _dot(yb, wb_ref[...])
                  + _dot(yc_ref[0], wc_ref[...]) + _dot(yd_ref[0], wd_ref[...]))


def _out_projection(hf, hb, rm, yc, yd, x, ml_norm, pool_w, pool_scale, w_out, tm=512):
    b, s, d = x.shape
    wa = w_out[0:256].reshape(N_HEADS, HEAD_DIM, d)
    wa = jnp.concatenate([wa, jnp.zeros_like(wa)], axis=1).reshape(N_HEADS * LANES, d).astype(BF16)
    wb, wc, wd = (w_out[256 * j:256 * (j + 1)].astype(BF16) for j in (1, 2, 3))
    gml = ml_norm.reshape(N_HEADS, HEAD_DIM)
    gml = jnp.concatenate([gml, jnp.zeros_like(gml)], axis=1).reshape(1, N_HEADS * LANES).astype(F32)
    wpool = jax.scipy.linalg.block_diag(*[pool_w[g] for g in range(4)]).astype(BF16)
    nb8 = tm // SUBLANES
    last8 = s // SUBLANES - 1
    row = lambda w, cb=0: pl.BlockSpec((1, tm, w), lambda bi, i: (bi, i, cb))
    in_specs = [
        row(512), row(512), row(512, RM_O // 512),
        row(256, RM_POOL // 256),
        pl.BlockSpec((1, SUBLANES, 256), lambda bi, i: (bi, jnp.maximum(i * nb8 - 1, 0), RM_POOL // 256)),
        pl.BlockSpec((1, SUBLANES, 256), lambda bi, i: (bi, jnp.minimum((i + 1) * nb8, last8), RM_POOL // 256)),
        row(256), row(256), row(d),
        _const_spec((1, 512)), _const_spec((256, 256)), _const_spec((1, 256)),
        _const_spec((512, d)), _const_spec((256, d)), _const_spec((256, d)), _const_spec((256, d)),
    ]
    kern = functools.partial(_outproj_kernel, tm=tm, seq=s)
    return pl.pallas_call(
        kern, grid=(b, s // tm), in_specs=in_specs, out_specs=row(d),
        out_shape=jax.ShapeDtypeStruct((b, s, d), F32),
        compiler_params=_params(2), name="out_projection",
    )(hf, hb, rm, rm, rm, rm, yc, yd, x, gml, wpool, pool_scale[None].astype(F32), wa, wb, wc, wd)


def _memkv_kernel(mem_ref, g_ref, w_ref, gk_ref, k_ref, v_ref):
    mn = _rms(mem_ref[0], g_ref[...]).astype(BF16)
    kv = _dot(mn, w_ref[...])
    ks = []
    for h in range(CROSS_HEADS):
        ks.append(_rms(kv[:, h * CROSS_HEAD_DIM:(h + 1) * CROSS_HEAD_DIM], gk_ref[...]))
    k_ref[0] = jnp.concatenate(ks, axis=1).astype(BF16)
    v_ref[0] = kv[:, D_MODEL:].astype(BF16)


def _mem_kv(mem, g_mem, w_kv, g_k):
    b, m, d = mem.shape
    blk = pl.BlockSpec((1, m, d), lambda bi: (bi, 0, 0))
    return pl.pallas_call(
        _memkv_kernel, grid=(b,),
        in_specs=[blk, _const_spec((1, d)), _const_spec((d, 2 * d)), _const_spec((1, CROSS_HEAD_DIM))],
        out_specs=(blk, blk),
        out_shape=(jax.ShapeDtypeStruct((b, m, d), BF16), jax.ShapeDtypeStruct((b, m, d), BF16)),
        compiler_params=_params(1), name="mem_kv",
    )(mem, g_mem[None].astype(F32), w_kv.astype(BF16), g_k[None].astype(F32))


def _cross_kernel(x_ref, g_ref, wq_ref, gq_ref, k_ref, v_ref, wo_ref, out_ref):
    x = x_ref[0]
    xn = _rms(x, g_ref[...]).astype(BF16)
    q = _dot(xn, wq_ref[...])
    outs = []
    for h in range(CROSS_HEADS):
        sl = slice(h * CROSS_HEAD_DIM, (h + 1) * CROSS_HEAD_DIM)
        qh = (_rms(q[:, sl], gq_ref[...]) * (CROSS_HEAD_DIM ** -0.5)).astype(BF16)
        s = _dot_nt(qh, k_ref[0, :, sl])
        p = jnp.exp(s - jnp.max(s, axis=-1, keepdims=True))
        l = jnp.sum(p, axis=-1, keepdims=True)
        outs.append((_dot(p.astype(BF16), v_ref[0, :, sl]) / l).astype(BF16))
    o = jnp.concatenate(outs, axis=1)
    out_ref[0] = x + _dot(o, wo_ref[...])


def _cross_attention(x, g, w_q, g_q, k, v, w_o, tm=512):
    b, s, d = x.shape
    m = k.shape[1]
    row = pl.BlockSpec((1, tm, d), lambda bi, i: (bi, i, 0))
    kvb = pl.BlockSpec((1, m, d), lambda bi, i: (bi, 0, 0))
    return pl.pallas_call(
        _cross_kernel, grid=(b, s // tm),
        in_specs=[row, _const_spec((1, d)), _const_spec((d, d)), _const_spec((1, CROSS_HEAD_DIM)),
                  kvb, kvb, _const_spec((d, d))],
        out_specs=row, out_shape=jax.ShapeDtypeStruct((b, s, d), F32),
        compiler_params=_params(2), name="cross_attention",
    )(x, g[None].astype(F32), w_q.astype(BF16), g_q[None].astype(F32), k, v, w_o.astype(BF16))


FF_CHUNK = 256


def _ffn_kernel(xm_ref, xp_ref, xn_ref, g_ref, win_ref, cw_ref, cb_ref, wout_ref, out_ref, *, tm):
    i = pl.program_id(1)
    n_t = pl.num_programs(1)
    xm = xm_ref[0]
    xe = jnp.concatenate([xp_ref[0], xm, xn_ref[0]], axis=0)
    hn = _rms(xe, g_ref[...]).astype(BF16)
    n = tm + 2 * SUBLANES
    core = slice(SUBLANES, SUBLANES + tm)
    hc = hn[core]
    r = lax.broadcasted_iota(jnp.int32, (n, 1), 0)
    live = jnp.logical_and(jnp.logical_or(i > 0, r >= SUBLANES),
                           jnp.logical_or(i < n_t - 1, r < SUBLANES + tm)).astype(F32)
    acc = xm
    for c in range(D_FF // FF_CHUNK):
        gs = slice(c * FF_CHUNK, (c + 1) * FF_CHUNK)
        us = slice(D_FF + c * FF_CHUNK, D_FF + (c + 1) * FF_CHUNK)
        gate = _dot(hn, win_ref[:, gs]) * live
        cw = cw_ref[:, gs]
        gconv = (pltpu.roll(gate, 1, 0)[core] * cw[0:1] + gate[core] * cw[1:2]
                 + pltpu.roll(gate, n - 1, 0)[core] * cw[2:3] + cb_ref[:, gs])
        up = _dot(hc, win_ref[:, us])
        act = (gconv / (1.0 + jnp.exp(-gconv)) * up).astype(BF16)
        acc = acc + _dot(act, wout_ref[gs, :])
    out_ref[0] = acc


def _conv_ffn(x, g, w_in, conv_w, conv_b, w_out, tm=512):
    b, s, d = x.shape
    nb8 = tm // SUBLANES
    last8 = s // SUBLANES - 1
    row = pl.BlockSpec((1, tm, d), lambda bi, i: (bi, i, 0))
    in_specs = [
        row,
        pl.BlockSpec((1, SUBLANES, d), lambda bi, i: (bi, jnp.maximum(i * nb8 - 1, 0), 0)),
        pl.BlockSpec((1, SUBLANES, d), lambda bi, i: (bi, jnp.minimum((i + 1) * nb8, last8), 0)),
        _const_spec((1, d)), _const_spec((d, 2 * D_FF)), _const_spec((3, D_FF)), _const_spec((1, D_FF)),
        _const_spec((D_FF, d)),
    ]
    return pl.pallas_call(
        functools.partial(_ffn_kernel, tm=tm), grid=(b, s // tm), in_specs=in_specs, out_specs=row,
        out_shape=jax.ShapeDtypeStruct((b, s, d), F32),
        compiler_params=_params(2), name="conv_ffn",
    )(x, x, x, g[None].astype(F32), w_in.astype(BF16), conv_w.astype(F32), conv_b[None].astype(F32),
      w_out.astype(BF16))


def _rope_tables(s):
    pos = jnp.arange(s)
    row = pos // GRID_W
    col = pos - row * GRID_W
    inv = ROPE_THETA ** (-jnp.arange(0, DIFF_QK_DIM, 2, dtype=F32) / DIFF_QK_DIM)
    out = []
    for p in (pos, row, col):
        ang = inv[:, None] * p.astype(F32)[None, :]
        out += [jnp.cos(ang), jnp.sin(ang)]
    return jnp.stack(out)


def kernel(x, mem, norm_mix, w_in, ml_bias_i, ml_bias_f, ml_norm, pool_w, pool_scale, diff_qnorm, diff_knorm, diff_lambda, diff_subnorm, gqa_qnorm, gqa_knorm, w_out, norm_cross, norm_mem, w_cq, w_ckv, cross_qnorm, cross_knorm, w_co, norm_ffn, w_ffn_in, ffn_conv, ffn_conv_b, w_ffn_out):
    depth = w_in.shape[0]
    tabs = _rope_tables(x.shape[1])
    for l in range(depth):
        lam_init = 0.8 - 0.6 * math.exp(-0.3 * l)
        w_rm, w_gate, w_t = _inproj_weights(w_in[l])
        (rm, gates, gates_t, qtd, kd, vtd, qtg, kg, vtg, mkt) = _in_projection(
            x, norm_mix[l], w_rm, w_gate, w_t, tabs, diff_qnorm[l], diff_knorm[l], gqa_qnorm[l], gqa_knorm[l])
        hf, hb = _mlstm(rm, mkt, gates, gates_t, ml_bias_i[l], ml_bias_f[l])
        yc = _diff_attention(qtd, kd, vtd, diff_lambda[l], diff_subnorm[l], lam_init,
                             _score_bound(diff_qnorm[l], diff_knorm[l], DIFF_QK_DIM))
        yd = _gqa_attention(qtg, kg, vtg, _score_bound(gqa_qnorm[l], gqa_knorm[l], HEAD_DIM))
        x = _out_projection(hf, hb, rm, yc, yd, x, ml_norm[l], pool_w[l], pool_scale[l], w_out[l])
        ck, cv = _mem_kv(mem, norm_mem[l], w_ckv[l], cross_knorm[l])
        x = _cross_attention(x, norm_cross[l], w_cq[l], cross_qnorm[l], ck, cv, w_co[l])
        x = _conv_ffn(x, norm_ffn[l], w_ffn_in[l], ffn_conv[l], ffn_conv_b[l], w_ffn_out[l])
    return x
```

```python
import functools
import math

import jax
import jax.numpy as jnp
from jax import lax
from jax.experimental import pallas as pl
from jax.experimental.pallas import tpu as pltpu

F32 = jnp.float32
BF16 = jnp.bfloat16
FP8 = jnp.float8_e4m3fn

D_MODEL = 1024
HEAD_DIM = 64
GROUP_WIDTH = 256
N_HEADS = 4
POOL_WINDOWS = (2, 4, 8, 16)
DIFF_QK_DIM = 32
GQA_KV_HEADS = 2
GRID_W = 64
ROPE_THETA = 10000.0
CROSS_HEADS = 4
CROSS_HEAD_DIM = 256
D_FF = 2816
EPS = 1e-6

LANES = 128
SUBLANES = 8
VMEM_LIMIT_BYTES = 56 * 1024 * 1024

V_AUG = HEAD_DIM + 16
ML_CHUNK = 128
NEG_BIG = -1e30
LOG2E = math.log2(math.e)
MAX_SCORE_BOUND = 50.0

_NT = (((1,), (1,)), ((), ()))


def _dot(a, b):
    return jnp.dot(a, b, preferred_element_type=F32)


def _dot_nt(a, b):
    return lax.dot_general(a, b, _NT, preferred_element_type=F32)


def _dot_f32(a, b):
    return jnp.dot(a, b, preferred_element_type=F32, precision=lax.Precision.HIGHEST)


def _params(n_axes):
    return pltpu.CompilerParams(dimension_semantics=("arbitrary",) * n_axes,
                                vmem_limit_bytes=VMEM_LIMIT_BYTES)


def _const_spec(shape):
    nd = len(shape)
    return pl.BlockSpec(shape, lambda *_: (0,) * nd, pipeline_mode=pl.Buffered(1))


def _rms(x, g):
    ms = jnp.mean(x * x, axis=-1, keepdims=True)
    return x * lax.rsqrt(ms + EPS) * g


RM_Q, RM_K, RM_V, RM_O, RM_POOL = 0, 512, 1024, 1536, 2048
RM_COLS = 2304
T_DQ, T_DK, T_GQ, T_GK, T_DV, T_GV, T_MK, T_GATE = 0, 256, 512, 768, 896, 1152, 1280, 1792
T_ROWS = 1808


def _norm_rope_t(z, norm_rows, gain, cos64, sin64, scale):
    r, t = z.shape
    zg = z.reshape(r // norm_rows, norm_rows, t)
    ms = jnp.mean(zg * zg, axis=1, keepdims=True)
    zn = (zg * lax.rsqrt(ms + EPS) * gain[None]).reshape(r // 64, 64, t)
    sw = jnp.concatenate([zn[:, 16:32], zn[:, 0:16], zn[:, 48:64], zn[:, 32:48]], axis=1)
    out = zn * cos64[None] + sw * sin64[None]
    return (out * scale).reshape(r, t)


FP8_SCALE_DIFF = 64.0
FP8_SCALE_GQA = 32.0


def _split_fp8(z, dim, is_query):
    r, t = z.shape
    z3 = z.reshape(r // dim, dim, t)
    hi = z3.astype(FP8).astype(F32)
    lo = (z3 - hi).astype(FP8).astype(F32)
    parts = [hi, hi, lo] if is_query else [hi, lo, hi]
    return jnp.concatenate(parts + [jnp.zeros_like(hi)], axis=1).reshape(4 * r, t)


def _with_ones_rows(vt, heads):
    t = vt.shape[1]
    v3 = vt.reshape(heads, HEAD_DIM, t)
    ones = jnp.ones((heads, V_AUG - HEAD_DIM, t), F32)
    return jnp.concatenate([v3, ones], axis=1).reshape(heads * V_AUG, t)


def _inproj_kernel(x_ref, g_ref, wrm_ref, wg_ref, wt_ref, cscale_ref, cadd_ref, tabs_ref,
                   gqd_ref, gkd_ref, gqg_ref, gkg_ref,
                   rm_ref, gates_ref, gates_t_ref, qtd_ref, kd_ref, vtd_ref,
                   qtg_ref, kg_ref, vtg_ref, mkt_ref):
    x = x_ref[0]
    h = _rms(x, g_ref[...]).astype(BF16)
    zr = _dot(h, wrm_ref[...])
    rm_ref[0] = (zr * cscale_ref[...] + cadd_ref[...]).astype(BF16)
    gates_ref[0] = _dot(h, wg_ref[...])
    zt = _dot_nt(wt_ref[...], h)

    tabs = tabs_ref[...]
    c1, s1, rc, rs, cc, cs = (tabs[i] for i in range(6))
    cos_d = jnp.concatenate([c1, c1, c1, c1], axis=0)
    sin_d = jnp.concatenate([-s1, s1, -s1, s1], axis=0)
    cos_g = jnp.concatenate([rc, rc, cc, cc], axis=0)
    sin_g = jnp.concatenate([-rs, rs, -cs, cs], axis=0)

    qd = _norm_rope_t(zt[T_DQ:T_DQ + 256], DIFF_QK_DIM, gqd_ref[...], cos_d, sin_d, FP8_SCALE_DIFF)
    qtd_ref[0] = _split_fp8(qd, DIFF_QK_DIM, True).astype(FP8)
    kd = _norm_rope_t(zt[T_DK:T_DK + 256], DIFF_QK_DIM, gkd_ref[...], cos_d, sin_d, FP8_SCALE_DIFF)
    kd_ref[0] = _split_fp8(kd, DIFF_QK_DIM, False).T.astype(FP8)
    qg = _norm_rope_t(zt[T_GQ:T_GQ + 256], HEAD_DIM, gqg_ref[...], cos_g, sin_g, FP8_SCALE_GQA)
    qtg_ref[0] = _split_fp8(qg, HEAD_DIM, True).astype(FP8)
    kg = _norm_rope_t(zt[T_GK:T_GK + 128], HEAD_DIM, gkg_ref[...], cos_g, sin_g, FP8_SCALE_GQA)
    kg_ref[0] = _split_fp8(kg, HEAD_DIM, False).T.astype(FP8)
    vtd_ref[0] = _with_ones_rows(zt[T_DV:T_DV + 256], N_HEADS).astype(BF16)
    vtg_ref[0] = _with_ones_rows(zt[T_GV:T_GV + 128], GQA_KV_HEADS).astype(BF16)
    mkt_ref[0] = (zt[T_MK:T_MK + 512] * (HEAD_DIM ** -0.5)).astype(BF16)
    gates_t_ref[0] = zt[T_GATE:T_GATE + 16]


def _pad_heads_cols(w):
    d = w.shape[0]
    w4 = w.reshape(d, N_HEADS, HEAD_DIM)
    return jnp.concatenate([w4, jnp.zeros_like(w4)], axis=2).reshape(d, N_HEADS * LANES)


def _inproj_weights(w_in):
    sizes = (256, 256, 256, 256, 16, 256, 256, 256, 256, 256, 128, 128)
    offs = [0]
    for s in sizes:
        offs.append(offs[-1] + s)
    (ml_q, ml_k, ml_v, ml_o, ml_g, pool, d_q, d_k, d_v, g_q, g_k, g_v) = (
        w_in[:, offs[i]:offs[i + 1]] for i in range(12))
    w_rm = jnp.concatenate([_pad_heads_cols(ml_q), _pad_heads_cols(ml_k), _pad_heads_cols(ml_v),
                            _pad_heads_cols(ml_o), pool], axis=1).astype(BF16)
    w_gate = jnp.concatenate([ml_g, jnp.zeros((D_MODEL, LANES - 16), F32)], axis=1).astype(BF16)
    w_t = jnp.concatenate([d_q, d_k, g_q, g_k, d_v, g_v, _pad_heads_cols(ml_k), ml_g], axis=1).T.astype(BF16)
    return w_rm, w_gate, w_t


def _inproj_col_consts():
    lane = jnp.arange(RM_COLS)
    in_k = (lane >= RM_K) & (lane < RM_V)
    cscale = jnp.where(in_k, HEAD_DIM ** -0.5, 1.0).astype(F32)[None]
    ones_col = (lane >= RM_V) & (lane < RM_O) & ((lane % LANES) == HEAD_DIM)
    cadd = jnp.where(ones_col, 1.0, 0.0).astype(F32)[None]
    return cscale, cadd


def _in_projection(x, g, w_rm, w_gate, w_t, tabs, gqd, gkd, gqg, gkg, tm=512):
    b, s, d = x.shape
    cscale, cadd = _inproj_col_consts()
    bc = lambda v: jnp.broadcast_to(v.astype(F32)[:, None], (v.shape[0], tm))
    grid = (b, s // tm)
    row_blk = lambda w: pl.BlockSpec((1, tm, w), lambda bi, i: (bi, i, 0))
    col_blk = lambda r: pl.BlockSpec((1, r, tm), lambda bi, i: (bi, 0, i))
    out_shapes = (
        jax.ShapeDtypeStruct((b, s, RM_COLS), BF16),
        jax.ShapeDtypeStruct((b, s, LANES), F32),
        jax.ShapeDtypeStruct((b, 16, s), F32),
        jax.ShapeDtypeStruct((b, 1024, s), FP8),
        jax.ShapeDtypeStruct((b, s, 1024), FP8),
        jax.ShapeDtypeStruct((b, N_HEADS * V_AUG, s), BF16),
        jax.ShapeDtypeStruct((b, 1024, s), FP8),
        jax.ShapeDtypeStruct((b, s, 512), FP8),
        jax.ShapeDtypeStruct((b, GQA_KV_HEADS * V_AUG, s), BF16),
        jax.ShapeDtypeStruct((b, 512, s), BF16),
    )
    out_specs = (row_blk(RM_COLS), row_blk(LANES), col_blk(16), col_blk(1024), row_blk(1024),
                 col_blk(N_HEADS * V_AUG), col_blk(1024), row_blk(512), col_blk(GQA_KV_HEADS * V_AUG),
                 col_blk(512))
    in_specs = [
        row_blk(d),
        _const_spec((1, d)),
        _const_spec(w_rm.shape), _const_spec(w_gate.shape), _const_spec(w_t.shape),
        _const_spec((1, RM_COLS)), _const_spec((1, RM_COLS)),
        pl.BlockSpec((6, 16, tm), lambda bi, i: (0, 0, i)),
        _const_spec((DIFF_QK_DIM, tm)), _const_spec((DIFF_QK_DIM, tm)),
        _const_spec((HEAD_DIM, tm)), _const_spec((HEAD_DIM, tm)),
    ]
    return pl.pallas_call(
        _inproj_kernel, grid=grid, in_specs=in_specs, out_specs=out_specs, out_shape=out_shapes,
        compiler_params=_params(2), name="in_projection",
    )(x, g[None].astype(F32), w_rm, w_gate, w_t, cscale, cadd, tabs, bc(gqd), bc(gkd), bc(gqg), bc(gkg))


def _attn_kernel(*refs, combos, width, tq, tk, n_extra, finalize, bounded):
    post_ref, qt_ref, k_ref, vt_ref = refs[:4]
    extra = refs[4:4 + n_extra]
    o_ref = refs[4 + n_extra]
    m_sc, acc_sc, p_sc = refs[5 + n_extra:]
    n_kv = k_ref.shape[1] // tk
    n_c = len(combos)
    post = post_ref[0, 0]

    m_sc[...] = jnp.full(m_sc.shape, NEG_BIG, F32)
    acc_sc[...] = jnp.zeros(acc_sc.shape, F32)

    def scores(c, start):
        q_row0, k_lane0 = combos[c][0], combos[c][1]
        st = _dot(k_ref[0, pl.ds(start, tk), k_lane0:k_lane0 + width], qt_ref[0, q_row0:q_row0 + width, :])
        return st * post

    def values(c, start):
        v_row0 = combos[c][2]
        return vt_ref[0, v_row0:v_row0 + V_AUG, pl.ds(start, tk)]

    def qk_exp(c, start, slot):
        p_sc[slot, c] = jnp.exp2(scores(c, start).astype(BF16))

    def pv(c, start, slot):
        acc_sc[c] += _dot(values(c, start), p_sc[slot, c])

    def bounded_pair(i, carry):
        s0 = pl.multiple_of(2 * i * tk, tk)
        s1 = pl.multiple_of((2 * i + 1) * tk, tk)
        s2 = pl.multiple_of((2 * i + 2) * tk, tk)
        for c in range(n_c):
            qk_exp(c, s1, 1)
            pv(c, s0, 0)
        for c in range(n_c):
            qk_exp(c, s2, 0)
            pv(c, s1, 1)
        return carry

    def online_body(j, carry):
        start = pl.multiple_of(j * tk, tk)
        for c in range(n_c):
            st = scores(c, start)
            m_prev = m_sc[c]
            m_new = jnp.maximum(m_prev, jnp.max(st, axis=0, keepdims=True))
            p = jnp.exp2(st - m_new).astype(BF16)
            acc_sc[c] = jnp.exp2(m_prev - m_new) * acc_sc[c] + _dot(values(c, start), p)
            m_sc[c] = m_new
        return carry

    if bounded:
        for c in range(n_c):
            qk_exp(c, 0, 0)
        lax.fori_loop(0, n_kv // 2 - 1, bounded_pair, 0)
        for c in range(n_c):
            qk_exp(c, (n_kv - 1) * tk, 1)
            pv(c, (n_kv - 2) * tk, 0)
        for c in range(n_c):
            pv(c, (n_kv - 1) * tk, 1)
    else:
        lax.fori_loop(0, n_kv, online_body, 0)
    out_t = finalize([acc_sc[c] for c in range(len(combos))], extra)
    o_ref[0] = out_t.T.astype(BF16)


def _finalize_diff(accs, extra, *, lam_init):
    lam_ref, gsub_ref = extra
    lp = lam_ref[...]
    lam = (jnp.exp(jnp.sum(lp[0:1] * lp[1:2], axis=1, keepdims=True))
           - jnp.exp(jnp.sum(lp[2:3] * lp[3:4], axis=1, keepdims=True)) + lam_init)
    outs = []
    for h in range(N_HEADS):
        a1, a2 = accs[2 * h], accs[2 * h + 1]
        o = (a1[:HEAD_DIM] / a1[HEAD_DIM:HEAD_DIM + 1]
             - lam * (a2[:HEAD_DIM] / a2[HEAD_DIM:HEAD_DIM + 1]))
        ms = jnp.mean(o * o, axis=0, keepdims=True)
        outs.append(o * lax.rsqrt(ms + EPS) * gsub_ref[...] * (1.0 - lam_init))
    return jnp.concatenate(outs, axis=0)


def _finalize_gqa(accs, extra):
    return jnp.concatenate([a[:HEAD_DIM] / a[HEAD_DIM:HEAD_DIM + 1] for a in accs], axis=0)


def _gain_max(g):
    return jnp.maximum(jnp.max(jnp.abs(g)), 1e-30)


def _post_scale(g_q, g_k, dim, fp8_scale):
    return _gain_max(g_q) * _gain_max(g_k) * (dim ** -0.5) * LOG2E / (fp8_scale * fp8_scale)


def _score_bound(g_q, g_k, dim):
    return 1.02 * (dim ** 0.5) * LOG2E * jnp.max(jnp.abs(g_q)) * jnp.max(jnp.abs(g_k))


def _attention(qt, k, vt, extra, bound, post, *, name, combos, width, finalize, tq=512, tk=512):
    b, _, s = qt.shape
    n_c = len(combos)
    in_specs = [
        pl.BlockSpec(memory_space=pltpu.SMEM),
        pl.BlockSpec((1, qt.shape[1], tq), lambda bi, i: (bi, 0, i)),
        pl.BlockSpec((1, s, k.shape[2]), lambda bi, i: (bi, 0, 0)),
        pl.BlockSpec((1, vt.shape[1], s), lambda bi, i: (bi, 0, 0)),
    ] + [_const_spec(e.shape) for e in extra]

    def call(bounded):
        kern = functools.partial(_attn_kernel, combos=combos, width=width, tq=tq, tk=tk,
                                 n_extra=len(extra), finalize=finalize, bounded=bounded)
        return pl.pallas_call(
            kern, grid=(b, s // tq), in_specs=in_specs,
            out_specs=pl.BlockSpec((1, tq, 256), lambda bi, i: (bi, i, 0)),
            out_shape=jax.ShapeDtypeStruct((b, s, 256), BF16),
            scratch_shapes=[pltpu.VMEM((n_c, 1, tq), F32),
                            pltpu.VMEM((n_c, V_AUG, tq), F32),
                            pltpu.VMEM((2, n_c, tk, tq) if bounded else (1, 1, 16, LANES), BF16)],
            compiler_params=_params(2),
            name="attention_%s_%s" % (name, "bounded" if bounded else "online"),
        )(post.reshape(1, 1).astype(F32), qt, k, vt, *extra)

    return lax.cond(bound <= MAX_SCORE_BOUND, lambda: call(True), lambda: call(False))


def _diff_attention(qt, k, vt, lam_params, g_sub, lam_init, bound, post, tq=512):
    combos = tuple((128 * (2 * h + c), 128 * (2 * h + c), V_AUG * h)
                   for h in range(N_HEADS) for c in range(2))
    gsub_b = jnp.broadcast_to(g_sub.astype(F32)[:, None], (HEAD_DIM, tq))
    fin = functools.partial(_finalize_diff, lam_init=lam_init)
    return _attention(qt, k, vt, (lam_params.astype(F32), gsub_b), bound, post, name="diff", combos=combos,
                      width=4 * DIFF_QK_DIM, finalize=fin, tq=tq)


def _gqa_attention(qt, k, vt, bound, post, tq=512):
    combos = tuple((256 * qh, 256 * (qh // 2), V_AUG * (qh // 2)) for qh in range(N_HEADS))
    return _attention(qt, k, vt, (), bound, post, name="gqa", combos=combos, width=4 * HEAD_DIM,
                      finalize=_finalize_gqa, tq=tq)


def _log_sigmoid(x):
    return jnp.minimum(x, 0.0) - jnp.log(1.0 + jnp.exp(-jnp.abs(x)))


def _mlstm_kernel(qf_ref, kf_ref, vf_ref, ktf_ref, gf_ref, gtf_ref,
                  qb_ref, kb_ref, vb_ref, ktb_ref, gb_ref, gtb_ref,
                  brow_ref, bcol_ref, hf_ref, hb_ref, c_sc, m_sc):
    L = ML_CHUNK

    @pl.when(pl.program_id(1) == 0)
    def _():
        c_sc[...] = jnp.zeros(c_sc.shape, F32)
        m_sc[...] = jnp.zeros(m_sc.shape, F32)

    row = lax.broadcasted_iota(jnp.int32, (L, L), 0)
    col = lax.broadcasted_iota(jnp.int32, (L, L), 1)
    lower = col <= row
    upper = col >= row
    lower_f = lower.astype(F32)
    upper_f = upper.astype(F32)
    lane = lax.broadcasted_iota(jnp.int32, (1, LANES), 1)
    forget_lane = (lane % 8) >= 4
    grow = lax.broadcasted_iota(jnp.int32, (16, 1), 0)
    forget_row = (grow % 8) >= 4
    keep = (lax.broadcasted_iota(jnp.int32, (1, LANES), 1) < HEAD_DIM).astype(F32)

    dirs = (
        (0, qf_ref, kf_ref, vf_ref, ktf_ref, gf_ref, gtf_ref, hf_ref, lower, lower_f, upper_f, L - 1),
        (1, qb_ref, kb_ref, vb_ref, ktb_ref, gb_ref, gtb_ref, hb_ref, upper, upper_f, lower_f, 0),
    )
    for d, q_ref, k_ref, v_ref, kt_ref, g_ref, gt_ref, h_ref, mask, tri_c, tri_r, last in dirs:
        g = g_ref[0] + brow_ref[...]
        gp = jnp.where(forget_lane, _log_sigmoid(g), g)
        gt = gt_ref[0] + bcol_ref[...]
        gtp = jnp.where(forget_row, _log_sigmoid(gt), gt)
        bcol_all = _dot_f32(tri_c, gp)
        brow_all = _dot_f32(gtp, tri_r)
        for h in range(N_HEADS):
            ii, fi = d * 8 + h, d * 8 + 4 + h
            idx = d * N_HEADS + h
            b_col = bcol_all[:, fi:fi + 1]
            b_row = brow_all[fi:fi + 1, :]
            c_row = gtp[ii:ii + 1, :] - b_row
            g_tot = b_row[:, last:last + 1]
            m_prev = m_sc[idx][0:1, 0:1]

            qh = q_ref[0, :, h * LANES:(h + 1) * LANES]
            kh = k_ref[0, :, h * LANES:(h + 1) * LANES]
            vh = v_ref[0, :, h * LANES:(h + 1) * LANES]
            kth = kt_ref[0, h * LANES:(h + 1) * LANES, :]

            cmask = jnp.where(mask, c_row, -jnp.inf)
            mu = jnp.maximum(m_prev, jnp.max(cmask, axis=1, keepdims=True))
            wts = (jnp.exp(cmask - mu) * _dot_nt(qh, kh)).astype(BF16)
            s_inter = jnp.exp(m_prev - mu)
            c_prev = c_sc[idx]
            tot = _dot(wts, vh) + s_inter * _dot(qh, c_prev.astype(BF16))
            den = tot[:, HEAD_DIM:HEAD_DIM + 1]
            hout = tot / jnp.maximum(jnp.abs(den), jnp.exp(-(b_col + mu)))
            h_ref[0, :, h * LANES:(h + 1) * LANES] = (hout * keep).astype(BF16)

            c_max = jnp.max(c_row, axis=1, keepdims=True)
            m_loc = g_tot + c_max
            w_row = jnp.exp(c_row - c_max)
            c_loc = _dot((kth.astype(F32) * w_row).astype(BF16), vh)
            m_new = jnp.maximum(g_tot + m_prev, m_loc)
            c_sc[idx] = jnp.exp(g_tot + m_prev - m_new) * c_prev + jnp.exp(m_loc - m_new) * c_loc
            m_sc[idx] = jnp.broadcast_to(m_new, (SUBLANES, LANES))


def _mlstm(rm, mkt, gates, gates_t, b_i, b_f):
    b, s, _ = rm.shape
    L = ML_CHUNK
    nc = s // L
    bias = jnp.stack([b_i[0], b_f[0], b_i[1], b_f[1]]).reshape(16).astype(F32)
    bias_row = jnp.concatenate([bias, jnp.zeros((LANES - 16,), F32)])[None]
    bias_col = jnp.broadcast_to(bias[:, None], (16, L))
    fwd = lambda cblk: (lambda bi, i: (bi, i, cblk))
    bwd = lambda cblk: (lambda bi, i: (bi, nc - 1 - i, cblk))
    in_specs = []
    for mk in (fwd, bwd):
        in_specs += [pl.BlockSpec((1, L, 512), mk(RM_Q // 512)),
                     pl.BlockSpec((1, L, 512), mk(RM_K // 512)),
                     pl.BlockSpec((1, L, 512), mk(RM_V // 512)),
                     pl.BlockSpec((1, 512, L), (lambda bi, i: (bi, 0, i)) if mk is fwd
                                  else (lambda bi, i: (bi, 0, nc - 1 - i))),
                     pl.BlockSpec((1, L, LANES), mk(0)),
                     pl.BlockSpec((1, 16, L), (lambda bi, i: (bi, 0, i)) if mk is fwd
                                  else (lambda bi, i: (bi, 0, nc - 1 - i)))]
    in_specs += [_const_spec((1, LANES)), _const_spec((16, L))]
    out_specs = (pl.BlockSpec((1, L, 512), lambda bi, i: (bi, i, 0)),
                 pl.BlockSpec((1, L, 512), lambda bi, i: (bi, nc - 1 - i, 0)))
    out_shape = (jax.ShapeDtypeStruct((b, s, 512), BF16), jax.ShapeDtypeStruct((b, s, 512), BF16))
    return pl.pallas_call(
        _mlstm_kernel, grid=(b, nc), in_specs=in_specs, out_specs=out_specs, out_shape=out_shape,
        scratch_shapes=[pltpu.VMEM((2 * N_HEADS, LANES, LANES), F32),
                        pltpu.VMEM((2 * N_HEADS, SUBLANES, LANES), F32)],
        compiler_params=_params(2), name="mlstm",
    )(rm, rm, rm, mkt, gates, gates_t, rm, rm, rm, mkt, gates, gates_t, bias_row, bias_col)


def _outproj_kernel(hf_ref, hb_ref, o_ref, pm_ref, pp_ref, pn_ref, yc_ref, yd_ref, x_ref,
                    gml_ref, wpool_ref, pscale_ref, wa_ref, wb_ref, wc_ref, wd_ref, out_ref, *, tm, seq):
    i = pl.program_id(1)
    n_t = pl.num_programs(1)

    hs = hf_ref[0].astype(F32) + hb_ref[0].astype(F32)
    og = o_ref[0].astype(F32)
    parts = []
    for h in range(N_HEADS):
        sl = slice(h * LANES, (h + 1) * LANES)
        hh = hs[:, sl]
        ms = jnp.sum(hh * hh, axis=-1, keepdims=True) * (1.0 / HEAD_DIM)
        gate = 1.0 / (1.0 + jnp.exp(-og[:, sl]))
        parts.append(hh * lax.rsqrt(ms + EPS) * gml_ref[:, sl] * gate)
    ya = jnp.concatenate(parts, axis=1).astype(BF16)

    prev = jnp.where(i > 0, pp_ref[0].astype(F32), 0.0)
    nxt = jnp.where(i < n_t - 1, pn_ref[0].astype(F32), 0.0)
    u = jnp.concatenate([prev, pm_ref[0].astype(F32), nxt], axis=0)
    n = tm + 2 * SUBLANES
    sh = lambda v, k: pltpu.roll(v, k % n, 0)
    w2 = u + sh(u, 1)
    w4 = sh(w2, 1) + sh(w2, -1)
    w8 = sh(w4, 2) + sh(w4, -2)
    w16 = sh(w8, 4) + sh(w8, -4)
    core = slice(SUBLANES, SUBLANES + tm)
    lane = lax.broadcasted_iota(jnp.int32, (1, GROUP_WIDTH), 1)
    t = (i * tm + lax.broadcasted_iota(jnp.int32, (tm, 1), 0)).astype(F32)
    total = jnp.where(lane < 64, w2[core], jnp.where(lane < 128, w4[core],
                      jnp.where(lane < 192, w8[core], w16[core])))
    half = jnp.where(lane < 64, 1.0, jnp.where(lane < 128, 2.0, jnp.where(lane < 192, 4.0, 8.0)))
    lo = jnp.maximum(t - half, 0.0)
    hi = jnp.minimum(t + half - 1.0, seq - 1.0)
    pooled = total / (hi - lo + 1.0) - u[core]
    yb = (_dot(pooled.astype(BF16), wpool_ref[...]) * pscale_ref[...]).astype(BF16)

    out_ref[0] = (x_ref[0] + _dot(ya, wa_ref[...]) + _dot(yb, wb_ref[...])
                  + _dot(yc_ref[0], wc_ref[...]) + _dot(yd_ref[0], wd_ref[...]))


def _out_projection(hf, hb, rm, yc, yd, x, ml_norm, pool_w, pool_scale, w_out, tm=512):
    b, s, d = x.shape
    wa = w_out[0:256].reshape(N_HEADS, HEAD_DIM, d)
    wa = jnp.concatenate([wa, jnp.zeros_like(wa)], axis=1).reshape(N_HEADS * LANES, d).astype(BF16)
    wb, wc, wd = (w_out[256 * j:256 * (j + 1)].astype(BF16) for j in (1, 2, 3))
    gml = ml_norm.reshape(N_HEADS, HEAD_DIM)
    gml = jnp.concatenate([gml, jnp.zeros_like(gml)], axis=1).reshape(1, N_HEADS * LANES).astype(F32)
    wpool = jax.scipy.linalg.block_diag(*[pool_w[g] for g in range(4)]).astype(BF16)
    nb8 = tm // SUBLANES
    last8 = s // SUBLANES - 1
    row = lambda w, cb=0: pl.BlockSpec((1, tm, w), lambda bi, i: (bi, i, cb))
    in_specs = [
        row(512), row(512), row(512, RM_O // 512),
        row(256, RM_POOL // 256),
        pl.BlockSpec((1, SUBLANES, 256), lambda bi, i: (bi, jnp.maximum(i * nb8 - 1, 0), RM_POOL // 256)),
        pl.BlockSpec((1, SUBLANES, 256), lambda bi, i: (bi, jnp.minimum((i + 1) * nb8, last8), RM_POOL // 256)),
        row(256), row(256), row(d),
        _const_spec((1, 512)), _const_spec((256, 256)), _const_spec((1, 256)),
        _const_spec((512, d)), _const_spec((256, d)), _const_spec((256, d)), _const_spec((256, d)),
    ]
    kern = functools.partial(_outproj_kernel, tm=tm, seq=s)
    return pl.pallas_call(
        kern, grid=(b, s // tm), in_specs=in_specs, out_specs=row(d),
        out_shape=jax.ShapeDtypeStruct((b, s, d), F32),
        compiler_params=_params(2), name="out_projection",
    )(hf, hb, rm, rm, rm, rm, yc, yd, x, gml, wpool, pool_scale[None].astype(F32), wa, wb, wc, wd)


def _memkv_kernel(mem_ref, g_ref, w_ref, gk_ref, k_ref, v_ref):
    mn = _rms(mem_ref[0], g_ref[...]).astype(BF16)
    kv = _dot(mn, w_ref[...])
    ks = []
    for h in range(CROSS_HEADS):
        ks.append(_rms(kv[:, h * CROSS_HEAD_DIM:(h + 1) * CROSS_HEAD_DIM], gk_ref[...]))
    k_ref[0] = jnp.concatenate(ks, axis=1).astype(BF16)
    v_ref[0] = kv[:, D_MODEL:].astype(BF16)


def _mem_kv(mem, g_mem, w_kv, g_k):
    b, m, d = mem.shape
    blk = pl.BlockSpec((1, m, d), lambda bi: (bi, 0, 0))
    return pl.pallas_call(
        _memkv_kernel, grid=(b,),
        in_specs=[blk, _const_spec((1, d)), _const_spec((d, 2 * d)), _const_spec((1, CROSS_HEAD_DIM))],
        out_specs=(blk, blk),
        out_shape=(jax.ShapeDtypeStruct((b, m, d), BF16), jax.ShapeDtypeStruct((b, m, d), BF16)),
        compiler_params=_params(1), name="mem_kv",
    )(mem, g_mem[None].astype(F32), w_kv.astype(BF16), g_k[None].astype(F32))


def _cross_kernel(x_ref, g_ref, wq_ref, gq_ref, k_ref, v_ref, wo_ref, out_ref):
    x = x_ref[0]
    xn = _rms(x, g_ref[...]).astype(BF16)
    q = _dot(xn, wq_ref[...])
    outs = []
    for h in range(CROSS_HEADS):
        sl = slice(h * CROSS_HEAD_DIM, (h + 1) * CROSS_HEAD_DIM)
        qh = (_rms(q[:, sl], gq_ref[...]) * (CROSS_HEAD_DIM ** -0.5)).astype(BF16)
        s = _dot_nt(qh, k_ref[0, :, sl])
        p = jnp.exp(s - jnp.max(s, axis=-1, keepdims=True))
        l = jnp.sum(p, axis=-1, keepdims=True)
        outs.append((_dot(p.astype(BF16), v_ref[0, :, sl]) / l).astype(BF16))
    o = jnp.concatenate(outs, axis=1)
    out_ref[0] = x + _dot(o, wo_ref[...])


def _cross_attention(x, g, w_q, g_q, k, v, w_o, tm=512):
    b, s, d = x.shape
    m = k.shape[1]
    row = pl.BlockSpec((1, tm, d), lambda bi, i: (bi, i, 0))
    kvb = pl.BlockSpec((1, m, d), lambda bi, i: (bi, 0, 0))
    return pl.pallas_call(
        _cross_kernel, grid=(b, s // tm),
        in_specs=[row, _const_spec((1, d)), _const_spec((d, d)), _const_spec((1, CROSS_HEAD_DIM)),
                  kvb, kvb, _const_spec((d, d))],
        out_specs=row, out_shape=jax.ShapeDtypeStruct((b, s, d), F32),
        compiler_params=_params(2), name="cross_attention",
    )(x, g[None].astype(F32), w_q.astype(BF16), g_q[None].astype(F32), k, v, w_o.astype(BF16))


FF_CHUNK = 256


def _ffn_kernel(xm_ref, xp_ref, xn_ref, g_ref, win_ref, cw_ref, cb_ref, wout_ref, out_ref, *, tm):
    i = pl.program_id(1)
    n_t = pl.num_programs(1)
    xm = xm_ref[0]
    xe = jnp.concatenate([xp_ref[0], xm, xn_ref[0]], axis=0)
    hn = _rms(xe, g_ref[...]).astype(BF16)
    n = tm + 2 * SUBLANES
    core = slice(SUBLANES, SUBLANES + tm)
    hc = hn[core]
    r = lax.broadcasted_iota(jnp.int32, (n, 1), 0)
    live = jnp.logical_and(jnp.logical_or(i > 0, r >= SUBLANES),
                           jnp.logical_or(i < n_t - 1, r < SUBLANES + tm)).astype(F32)
    acc = xm
    for c in range(D_FF // FF_CHUNK):
        gs = slice(c * FF_CHUNK, (c + 1) * FF_CHUNK)
        us = slice(D_FF + c * FF_CHUNK, D_FF + (c + 1) * FF_CHUNK)
        gate = _dot(hn, win_ref[:, gs]) * live
        cw = cw_ref[:, gs]
        gconv = (pltpu.roll(gate, 1, 0)[core] * cw[0:1] + gate[core] * cw[1:2]
                 + pltpu.roll(gate, n - 1, 0)[core] * cw[2:3] + cb_ref[:, gs])
        up = _dot(hc, win_ref[:, us])
        act = (gconv / (1.0 + jnp.exp(-gconv)) * up).astype(BF16)
        acc = acc + _dot(act, wout_ref[gs, :])
    out_ref[0] = acc


def _conv_ffn(x, g, w_in, conv_w, conv_b, w_out, tm=512):
    b, s, d = x.shape
    nb8 = tm // SUBLANES
    last8 = s // SUBLANES - 1
    row = pl.BlockSpec((1, tm, d), lambda bi, i: (bi, i, 0))
    in_specs = [
        row,
        pl.BlockSpec((1, SUBLANES, d), lambda bi, i: (bi, jnp.maximum(i * nb8 - 1, 0), 0)),
        pl.BlockSpec((1, SUBLANES, d), lambda bi, i: (bi, jnp.minimum((i + 1) * nb8, last8), 0)),
        _const_spec((1, d)), _const_spec((d, 2 * D_FF)), _const_spec((3, D_FF)), _const_spec((1, D_FF)),
        _const_spec((D_FF, d)),
    ]
    return pl.pallas_call(
        functools.partial(_ffn_kernel, tm=tm), grid=(b, s // tm), in_specs=in_specs, out_specs=row,
        out_shape=jax.ShapeDtypeStruct((b, s, d), F32),
        compiler_params=_params(2), name="conv_ffn",
    )(x, x, x, g[None].astype(F32), w_in.astype(BF16), conv_w.astype(F32), conv_b[None].astype(F32),
      w_out.astype(BF16))


def _rope_tables(s):
    pos = jnp.arange(s)
    row = pos // GRID_W
    col = pos - row * GRID_W
    inv = ROPE_THETA ** (-jnp.arange(0, DIFF_QK_DIM, 2, dtype=F32) / DIFF_QK_DIM)
    out = []
    for p in (pos, row, col):
        ang = inv[:, None] * p.astype(F32)[None, :]
        out += [jnp.cos(ang), jnp.sin(ang)]
    return jnp.stack(out)


def kernel(x, mem, norm_mix, w_in, ml_bias_i, ml_bias_f, ml_norm, pool_w, pool_scale, diff_qnorm, diff_knorm, diff_lambda, diff_subnorm, gqa_qnorm, gqa_knorm, w_out, norm_cross, norm_mem, w_cq, w_ckv, cross_qnorm, cross_knorm, w_co, norm_ffn, w_ffn_in, ffn_conv, ffn_conv_b, w_ffn_out):
    depth = w_in.shape[0]
    tabs = _rope_tables(x.shape[1])
    for l in range(depth):
        lam_init = 0.8 - 0.6 * math.exp(-0.3 * l)
        w_rm, w_gate, w_t = _inproj_weights(w_in[l])
        (rm, gates, gates_t, qtd, kd, vtd, qtg, kg, vtg, mkt) = _in_projection(
            x, norm_mix[l], w_rm, w_gate, w_t, tabs, diff_qnorm[l] / _gain_max(diff_qnorm[l]),
            diff_knorm[l] / _gain_max(diff_knorm[l]), gqa_qnorm[l] / _gain_max(gqa_qnorm[l]),
            gqa_knorm[l] / _gain_max(gqa_knorm[l]))
        hf, hb = _mlstm(rm, mkt, gates, gates_t, ml_bias_i[l], ml_bias_f[l])
        yc = _diff_attention(qtd, kd, vtd, diff_lambda[l], diff_subnorm[l], lam_init,
                             _score_bound(diff_qnorm[l], diff_knorm[l], DIFF_QK_DIM),
                             _post_scale(diff_qnorm[l], diff_knorm[l], DIFF_QK_DIM, FP8_SCALE_DIFF))
        yd = _gqa_attention(qtg, kg, vtg, _score_bound(gqa_qnorm[l], gqa_knorm[l], HEAD_DIM),
                            _post_scale(gqa_qnorm[l], gqa_knorm[l], HEAD_DIM, FP8_SCALE_GQA))
        x = _out_projection(hf, hb, rm, yc, yd, x, ml_norm[l], pool_w[l], pool_scale[l], w_out[l])
        ck, cv = _mem_kv(mem, norm_mem[l], w_ckv[l], cross_knorm[l])
        x = _cross_attention(x, norm_cross[l], w_cq[l], cross_qnorm[l], ck, cv, w_co[l])
        x = _conv_ffn(x, norm_ffn[l], w_ffn_in[l], ffn_conv[l], ffn_conv_b[l], w_ffn_out[l])
    return x
```

```python
import functools
import math

import jax
import jax.numpy as jnp
from jax import lax
from jax.experimental import pallas as pl
from jax.experimental.pallas import tpu as pltpu

F32 = jnp.float32
BF16 = jnp.bfloat16
FP8 = jnp.float8_e4m3fn

D_MODEL = 1024
HEAD_DIM = 64
GROUP_WIDTH = 256
N_HEADS = 4
POOL_WINDOWS = (2, 4, 8, 16)
DIFF_QK_DIM = 32
GQA_KV_HEADS = 2
GRID_W = 64
ROPE_THETA = 10000.0
CROSS_HEADS = 4
CROSS_HEAD_DIM = 256
D_FF = 2816
EPS = 1e-6

LANES = 128
SUBLANES = 8
VMEM_LIMIT_BYTES = 56 * 1024 * 1024

V_AUG = HEAD_DIM + 16
ML_CHUNK = 128
ML_SUB = 2
NEG_BIG = -1e30
LOG2E = math.log2(math.e)
MAX_SCORE_BOUND = 50.0

_NT = (((1,), (1,)), ((), ()))


def _dot(a, b):
    return jnp.dot(a, b, preferred_element_type=F32)


def _dot_nt(a, b):
    return lax.dot_general(a, b, _NT, preferred_element_type=F32)


def _dot_f32(a, b):
    return jnp.dot(a, b, preferred_element_type=F32, precision=lax.Precision.HIGHEST)


def _params(n_axes):
    return pltpu.CompilerParams(dimension_semantics=("arbitrary",) * n_axes,
                                vmem_limit_bytes=VMEM_LIMIT_BYTES)


def _const_spec(shape):
    nd = len(shape)
    return pl.BlockSpec(shape, lambda *_: (0,) * nd, pipeline_mode=pl.Buffered(1))


def _rms(x, g):
    ms = jnp.mean(x * x, axis=-1, keepdims=True)
    return x * lax.rsqrt(ms + EPS) * g


RM_Q, RM_K, RM_V, RM_O, RM_POOL = 0, 512, 1024, 1536, 2048
RM_COLS = 2304
T_DQ, T_DK, T_GQ, T_GK, T_DV, T_GV, T_MK, T_GATE = 0, 256, 512, 768, 896, 1152, 1280, 1536
T_ROWS = 1552


def _norm_rope_t(z, norm_rows, gain, cos64, sin64, scale):
    r, t = z.shape
    zg = z.reshape(r // norm_rows, norm_rows, t)
    ms = jnp.mean(zg * zg, axis=1, keepdims=True)
    zn = (zg * lax.rsqrt(ms + EPS) * gain[None]).reshape(r // 64, 64, t)
    sw = jnp.concatenate([zn[:, 16:32], zn[:, 0:16], zn[:, 48:64], zn[:, 32:48]], axis=1)
    out = zn * cos64[None] + sw * sin64[None]
    return (out * scale).reshape(r, t)


FP8_SCALE_DIFF = 64.0
FP8_SCALE_GQA = 32.0


def _split_fp8(z, dim, is_query):
    r, t = z.shape
    z3 = z.reshape(r // dim, dim, t)
    hi = z3.astype(FP8).astype(F32)
    lo = (z3 - hi).astype(FP8).astype(F32)
    parts = [hi, hi, lo] if is_query else [hi, lo, hi]
    return jnp.concatenate(parts + [jnp.zeros_like(hi)], axis=1).reshape(4 * r, t)


def _with_ones_rows(vt, heads):
    t = vt.shape[1]
    v3 = vt.reshape(heads, HEAD_DIM, t)
    ones = jnp.ones((heads, V_AUG - HEAD_DIM, t), F32)
    return jnp.concatenate([v3, ones], axis=1).reshape(heads * V_AUG, t)


def _pad_heads_lanes(z):
    low = lax.broadcasted_iota(jnp.int32, (1, LANES), 1) < HEAD_DIM
    tiles = []
    for j in range(z.shape[1] // LANES):
        tile = z[:, j * LANES:(j + 1) * LANES]
        tiles += [jnp.where(low, tile, 0.0), jnp.where(low, pltpu.roll(tile, HEAD_DIM, 1), 0.0)]
    return jnp.concatenate(tiles, axis=1)


def _pad_heads_rows(zt):
    t = zt.shape[1]
    z3 = zt.reshape(N_HEADS, HEAD_DIM, t)
    return jnp.concatenate([z3, jnp.zeros_like(z3)], axis=1).reshape(N_HEADS * LANES, t)


def _inproj_kernel(x_ref, g_ref, wrm_ref, wg_ref, wt_ref, cscale_ref, cadd_ref, tabs_ref,
                   gqd_ref, gkd_ref, gqg_ref, gkg_ref,
                   rm_ref, gates_ref, gates_t_ref, qtd_ref, kd_ref, vtd_ref,
                   qtg_ref, kg_ref, vtg_ref, mkt_ref):
    x = x_ref[0]
    h = _rms(x, g_ref[...]).astype(BF16)
    zr = _dot(h, wrm_ref[...])
    padded = [_pad_heads_lanes(zr[:, j * GROUP_WIDTH:(j + 1) * GROUP_WIDTH]) for j in range(4)]
    slab = jnp.concatenate(padded + [zr[:, 4 * GROUP_WIDTH:]], axis=1)
    rm_ref[0] = (slab * cscale_ref[...] + cadd_ref[...]).astype(BF16)
    gates_ref[0] = _dot(h, wg_ref[...])
    zt = _dot_nt(wt_ref[...], h)

    tabs = tabs_ref[...]
    c1, s1, rc, rs, cc, cs = (tabs[i] for i in range(6))
    cos_d = jnp.concatenate([c1, c1, c1, c1], axis=0)
    sin_d = jnp.concatenate([-s1, s1, -s1, s1], axis=0)
    cos_g = jnp.concatenate([rc, rc, cc, cc], axis=0)
    sin_g = jnp.concatenate([-rs, rs, -cs, cs], axis=0)

    qd = _norm_rope_t(zt[T_DQ:T_DQ + 256], DIFF_QK_DIM, gqd_ref[...], cos_d, sin_d, FP8_SCALE_DIFF)
    qtd_ref[0] = _split_fp8(qd, DIFF_QK_DIM, True).astype(FP8)
    kd = _norm_rope_t(zt[T_DK:T_DK + 256], DIFF_QK_DIM, gkd_ref[...], cos_d, sin_d, FP8_SCALE_DIFF)
    kd_ref[0] = _split_fp8(kd, DIFF_QK_DIM, False).T.astype(FP8)
    qg = _norm_rope_t(zt[T_GQ:T_GQ + 256], HEAD_DIM, gqg_ref[...], cos_g, sin_g, FP8_SCALE_GQA)
    qtg_ref[0] = _split_fp8(qg, HEAD_DIM, True).astype(FP8)
    kg = _norm_rope_t(zt[T_GK:T_GK + 128], HEAD_DIM, gkg_ref[...], cos_g, sin_g, FP8_SCALE_GQA)
    kg_ref[0] = _split_fp8(kg, HEAD_DIM, False).T.astype(FP8)
    vtd_ref[0] = _with_ones_rows(zt[T_DV:T_DV + 256], N_HEADS).astype(BF16)
    vtg_ref[0] = _with_ones_rows(zt[T_GV:T_GV + 128], GQA_KV_HEADS).astype(BF16)
    mkt_ref[0] = (_pad_heads_rows(zt[T_MK:T_MK + 256]) * (HEAD_DIM ** -0.5)).astype(BF16)
    gates_t_ref[0] = zt[T_GATE:T_GATE + 16]


def _inproj_weights(w_in):
    sizes = (256, 256, 256, 256, 16, 256, 256, 256, 256, 256, 128, 128)
    offs = [0]
    for s in sizes:
        offs.append(offs[-1] + s)
    (ml_q, ml_k, ml_v, ml_o, ml_g, pool, d_q, d_k, d_v, g_q, g_k, g_v) = (
        w_in[:, offs[i]:offs[i + 1]] for i in range(12))
    w_rm = jnp.concatenate([ml_q, ml_k, ml_v, ml_o, pool], axis=1).astype(BF16)
    w_gate = jnp.concatenate([ml_g, jnp.zeros((D_MODEL, LANES - 16), F32)], axis=1).astype(BF16)
    w_t = jnp.concatenate([d_q, d_k, g_q, g_k, d_v, g_v, ml_k, ml_g], axis=1).T.astype(BF16)
    return w_rm, w_gate, w_t


def _inproj_col_consts():
    lane = jnp.arange(RM_COLS)
    in_k = (lane >= RM_K) & (lane < RM_V)
    cscale = jnp.where(in_k, HEAD_DIM ** -0.5, 1.0).astype(F32)[None]
    ones_col = (lane >= RM_V) & (lane < RM_O) & ((lane % LANES) == HEAD_DIM)
    cadd = jnp.where(ones_col, 1.0, 0.0).astype(F32)[None]
    return cscale, cadd


def _in_projection(x, g, w_rm, w_gate, w_t, tabs, gqd, gkd, gqg, gkg, tm=512):
    b, s, d = x.shape
    cscale, cadd = _inproj_col_consts()
    bc = lambda v: jnp.broadcast_to(v.astype(F32)[:, None], (v.shape[0], tm))
    grid = (b, s // tm)
    row_blk = lambda w: pl.BlockSpec((1, tm, w), lambda bi, i: (bi, i, 0))
    col_blk = lambda r: pl.BlockSpec((1, r, tm), lambda bi, i: (bi, 0, i))
    out_shapes = (
        jax.ShapeDtypeStruct((b, s, RM_COLS), BF16),
        jax.ShapeDtypeStruct((b, s, LANES), F32),
        jax.ShapeDtypeStruct((b, 16, s), F32),
        jax.ShapeDtypeStruct((b, 1024, s), FP8),
        jax.ShapeDtypeStruct((b, s, 1024), FP8),
        jax.ShapeDtypeStruct((b, N_HEADS * V_AUG, s), BF16),
        jax.ShapeDtypeStruct((b, 1024, s), FP8),
        jax.ShapeDtypeStruct((b, s, 512), FP8),
        jax.ShapeDtypeStruct((b, GQA_KV_HEADS * V_AUG, s), BF16),
        jax.ShapeDtypeStruct((b, 512, s), BF16),
    )
    out_specs = (row_blk(RM_COLS), row_blk(LANES), col_blk(16), col_blk(1024), row_blk(1024),
                 col_blk(N_HEADS * V_AUG), col_blk(1024), row_blk(512), col_blk(GQA_KV_HEADS * V_AUG),
                 col_blk(512))
    in_specs = [
        row_blk(d),
        _const_spec((1, d)),
        _const_spec(w_rm.shape), _const_spec(w_gate.shape), _const_spec(w_t.shape),
        _const_spec((1, RM_COLS)), _const_spec((1, RM_COLS)),
        pl.BlockSpec((6, 16, tm), lambda bi, i: (0, 0, i)),
        _const_spec((DIFF_QK_DIM, tm)), _const_spec((DIFF_QK_DIM, tm)),
        _const_spec((HEAD_DIM, tm)), _const_spec((HEAD_DIM, tm)),
    ]
    return pl.pallas_call(
        _inproj_kernel, grid=grid, in_specs=in_specs, out_specs=out_specs, out_shape=out_shapes,
        compiler_params=_params(2), name="in_projection",
    )(x, g[None].astype(F32), w_rm, w_gate, w_t, cscale, cadd, tabs, bc(gqd), bc(gkd), bc(gqg), bc(gkg))


def _attn_kernel(*refs, combos, width, tq, tk, n_extra, finalize, bounded):
    post_ref, qt_ref, k_ref, vt_ref = refs[:4]
    extra = refs[4:4 + n_extra]
    o_ref = refs[4 + n_extra]
    m_sc, acc_sc, p_sc = refs[5 + n_extra:]
    n_kv = k_ref.shape[1] // tk
    n_c = len(combos)
    post = post_ref[0, 0]

    m_sc[...] = jnp.full(m_sc.shape, NEG_BIG, F32)
    acc_sc[...] = jnp.zeros(acc_sc.shape, F32)

    def scores(c, start):
        q_row0, k_lane0 = combos[c][0], combos[c][1]
        st = _dot(k_ref[0, pl.ds(start, tk), k_lane0:k_lane0 + width], qt_ref[0, q_row0:q_row0 + width, :])
        return st * post

    def values(c, start):
        v_row0 = combos[c][2]
        return vt_ref[0, v_row0:v_row0 + V_AUG, pl.ds(start, tk)]

    def qk_exp(c, start, slot):
        p_sc[slot, c] = jnp.exp2(scores(c, start).astype(BF16))

    def pv(c, start, slot):
        acc_sc[c] += _dot(values(c, start), p_sc[slot, c])

    def bounded_pair(i, carry):
        s0 = pl.multiple_of(2 * i * tk, tk)
        s1 = pl.multiple_of((2 * i + 1) * tk, tk)
        s2 = pl.multiple_of((2 * i + 2) * tk, tk)
        for c in range(n_c):
            qk_exp(c, s1, 1)
            pv(c, s0, 0)
        for c in range(n_c):
            qk_exp(c, s2, 0)
            pv(c, s1, 1)
        return carry

    def online_body(j, carry):
        start = pl.multiple_of(j * tk, tk)
        for c in range(n_c):
            st = scores(c, start)
            m_prev = m_sc[c]
            m_new = jnp.maximum(m_prev, jnp.max(st, axis=0, keepdims=True))
            p = jnp.exp2(st - m_new).astype(BF16)
            acc_sc[c] = jnp.exp2(m_prev - m_new) * acc_sc[c] + _dot(values(c, start), p)
            m_sc[c] = m_new
        return carry

    if bounded:
        for c in range(n_c):
            qk_exp(c, 0, 0)
        lax.fori_loop(0, n_kv // 2 - 1, bounded_pair, 0)
        for c in range(n_c):
            qk_exp(c, (n_kv - 1) * tk, 1)
            pv(c, (n_kv - 2) * tk, 0)
        for c in range(n_c):
            pv(c, (n_kv - 1) * tk, 1)
    else:
        lax.fori_loop(0, n_kv, online_body, 0)
    out_t = finalize([acc_sc[c] for c in range(len(combos))], extra)
    o_ref[0] = out_t.T.astype(BF16)


def _finalize_diff(accs, extra, *, lam_init):
    lam_ref, gsub_ref = extra
    lp = lam_ref[...]
    lam = (jnp.exp(jnp.sum(lp[0:1] * lp[1:2], axis=1, keepdims=True))
           - jnp.exp(jnp.sum(lp[2:3] * lp[3:4], axis=1, keepdims=True)) + lam_init)
    outs = []
    for h in range(N_HEADS):
        a1, a2 = accs[2 * h], accs[2 * h + 1]
        o = (a1[:HEAD_DIM] / a1[HEAD_DIM:HEAD_DIM + 1]
             - lam * (a2[:HEAD_DIM] / a2[HEAD_DIM:HEAD_DIM + 1]))
        ms = jnp.mean(o * o, axis=0, keepdims=True)
        outs.append(o * lax.rsqrt(ms + EPS) * gsub_ref[...] * (1.0 - lam_init))
    return jnp.concatenate(outs, axis=0)


def _finalize_gqa(accs, extra):
    return jnp.concatenate([a[:HEAD_DIM] / a[HEAD_DIM:HEAD_DIM + 1] for a in accs], axis=0)


def _gain_max(g):
    return jnp.maximum(jnp.max(jnp.abs(g)), 1e-30)


def _post_scale(g_q, g_k, dim, fp8_scale):
    return _gain_max(g_q) * _gain_max(g_k) * (dim ** -0.5) * LOG2E / (fp8_scale * fp8_scale)


def _score_bound(g_q, g_k, dim):
    return 1.02 * (dim ** 0.5) * LOG2E * jnp.max(jnp.abs(g_q)) * jnp.max(jnp.abs(g_k))


def _attention(qt, k, vt, extra, bound, post, *, name, combos, width, finalize, tq=512, tk=512):
    b, _, s = qt.shape
    n_c = len(combos)
    in_specs = [
        pl.BlockSpec(memory_space=pltpu.SMEM),
        pl.BlockSpec((1, qt.shape[1], tq), lambda bi, i: (bi, 0, i)),
        pl.BlockSpec((1, s, k.shape[2]), lambda bi, i: (bi, 0, 0)),
        pl.BlockSpec((1, vt.shape[1], s), lambda bi, i: (bi, 0, 0)),
    ] + [_const_spec(e.shape) for e in extra]

    def call(bounded):
        kern = functools.partial(_attn_kernel, combos=combos, width=width, tq=tq, tk=tk,
                                 n_extra=len(extra), finalize=finalize, bounded=bounded)
        return pl.pallas_call(
            kern, grid=(b, s // tq), in_specs=in_specs,
            out_specs=pl.BlockSpec((1, tq, 256), lambda bi, i: (bi, i, 0)),
            out_shape=jax.ShapeDtypeStruct((b, s, 256), BF16),
            scratch_shapes=[pltpu.VMEM((n_c, 1, tq), F32),
                            pltpu.VMEM((n_c, V_AUG, tq), F32),
                            pltpu.VMEM((2, n_c, tk, tq) if bounded else (1, 1, 16, LANES), BF16)],
            compiler_params=_params(2),
            name="attention_%s_%s" % (name, "bounded" if bounded else "online"),
        )(post.reshape(1, 1).astype(F32), qt, k, vt, *extra)

    return lax.cond(bound <= MAX_SCORE_BOUND, lambda: call(True), lambda: call(False))


def _diff_attention(qt, k, vt, lam_params, g_sub, lam_init, bound, post, tq=512):
    combos = tuple((128 * (2 * h + c), 128 * (2 * h + c), V_AUG * h)
                   for h in range(N_HEADS) for c in range(2))
    gsub_b = jnp.broadcast_to(g_sub.astype(F32)[:, None], (HEAD_DIM, tq))
    fin = functools.partial(_finalize_diff, lam_init=lam_init)
    return _attention(qt, k, vt, (lam_params.astype(F32), gsub_b), bound, post, name="diff", combos=combos,
                      width=4 * DIFF_QK_DIM, finalize=fin, tq=tq)


def _gqa_attention(qt, k, vt, bound, post, tq=512):
    combos = tuple((256 * qh, 256 * (qh // 2), V_AUG * (qh // 2)) for qh in range(N_HEADS))
    return _attention(qt, k, vt, (), bound, post, name="gqa", combos=combos, width=4 * HEAD_DIM,
                      finalize=_finalize_gqa, tq=tq)


def _log_sigmoid(x):
    return jnp.minimum(x, 0.0) - jnp.log(1.0 + jnp.exp(-jnp.abs(x)))


def _mlstm_kernel(qf_ref, kf_ref, vf_ref, ktf_ref, gf_ref, gtf_ref,
                  qb_ref, kb_ref, vb_ref, ktb_ref, gb_ref, gtb_ref,
                  brow_ref, bcol_ref, hf_ref, hb_ref, c_sc, m_sc):
    L = ML_CHUNK

    @pl.when(pl.program_id(1) == 0)
    def _():
        c_sc[...] = jnp.zeros(c_sc.shape, F32)
        m_sc[...] = jnp.zeros(m_sc.shape, F32)

    row = lax.broadcasted_iota(jnp.int32, (L, L), 0)
    col = lax.broadcasted_iota(jnp.int32, (L, L), 1)
    lower = col <= row
    upper = col >= row
    lower_f = lower.astype(F32)
    upper_f = upper.astype(F32)
    lane = lax.broadcasted_iota(jnp.int32, (1, LANES), 1)
    forget_lane = (lane % 8) >= 4
    grow = lax.broadcasted_iota(jnp.int32, (16, 1), 0)
    forget_row = (grow % 8) >= 4
    keep = (lax.broadcasted_iota(jnp.int32, (1, LANES), 1) < HEAD_DIM).astype(F32)

    fwd = (0, qf_ref, kf_ref, vf_ref, ktf_ref, gf_ref, gtf_ref, hf_ref, lower, lower_f, upper_f, L - 1)
    bwd = (1, qb_ref, kb_ref, vb_ref, ktb_ref, gb_ref, gtb_ref, hb_ref, upper, upper_f, lower_f, 0)
    work = []
    for j in range(ML_SUB):
        work += [fwd + (j * L,), bwd + ((ML_SUB - 1 - j) * L,)]
    for d, q_ref, k_ref, v_ref, kt_ref, g_ref, gt_ref, h_ref, mask, tri_c, tri_r, last, r0 in work:
        rows = slice(r0, r0 + L)
        g = g_ref[0, rows, :] + brow_ref[...]
        gp = jnp.where(forget_lane, _log_sigmoid(g), g)
        gt = gt_ref[0, :, rows] + bcol_ref[...]
        gtp = jnp.where(forget_row, _log_sigmoid(gt), gt)
        bcol_all = _dot_f32(tri_c, gp)
        brow_all = _dot_f32(gtp, tri_r)
        for h in range(N_HEADS):
            ii, fi = d * 8 + h, d * 8 + 4 + h
            idx = d * N_HEADS + h
            b_col = bcol_all[:, fi:fi + 1]
            b_row = brow_all[fi:fi + 1, :]
            c_row = gtp[ii:ii + 1, :] - b_row
            g_tot = b_row[:, last:last + 1]
            m_prev = m_sc[idx][0:1, 0:1]

            qh = q_ref[0, rows, h * LANES:(h + 1) * LANES]
            kh = k_ref[0, rows, h * LANES:(h + 1) * LANES]
            vh = v_ref[0, rows, h * LANES:(h + 1) * LANES]
            kth = kt_ref[0, h * LANES:(h + 1) * LANES, rows]

            cmask = jnp.where(mask, c_row, -jnp.inf)
            mu = jnp.maximum(m_prev, jnp.max(cmask, axis=1, keepdims=True))
            wts = (jnp.exp(cmask - mu) * _dot_nt(qh, kh)).astype(BF16)
            s_inter = jnp.exp(m_prev - mu)
            c_prev = c_sc[idx]
            tot = _dot(wts, vh) + s_inter * _dot(qh, c_prev.astype(BF16))
            den = tot[:, HEAD_DIM:HEAD_DIM + 1]
            hout = tot / jnp.maximum(jnp.abs(den), jnp.exp(-(b_col + mu)))
            h_ref[0, rows, h * LANES:(h + 1) * LANES] = (hout * keep).astype(BF16)

            c_max = jnp.max(c_row, axis=1, keepdims=True)
            m_loc = g_tot + c_max
            w_row = jnp.exp(c_row - c_max)
            c_loc = _dot((kth.astype(F32) * w_row).astype(BF16), vh)
            m_new = jnp.maximum(g_tot + m_prev, m_loc)
            c_sc[idx] = jnp.exp(g_tot + m_prev - m_new) * c_prev + jnp.exp(m_loc - m_new) * c_loc
            m_sc[idx] = jnp.broadcast_to(m_new, (SUBLANES, LANES))


def _mlstm(rm, mkt, gates, gates_t, b_i, b_f):
    b, s, _ = rm.shape
    L = ML_CHUNK * ML_SUB
    nc = s // L
    bias = jnp.stack([b_i[0], b_f[0], b_i[1], b_f[1]]).reshape(16).astype(F32)
    bias_row = jnp.concatenate([bias, jnp.zeros((LANES - 16,), F32)])[None]
    bias_col = jnp.broadcast_to(bias[:, None], (16, ML_CHUNK))
    fwd = lambda cblk: (lambda bi, i: (bi, i, cblk))
    bwd = lambda cblk: (lambda bi, i: (bi, nc - 1 - i, cblk))
    in_specs = []
    for mk in (fwd, bwd):
        in_specs += [pl.BlockSpec((1, L, 512), mk(RM_Q // 512)),
                     pl.BlockSpec((1, L, 512), mk(RM_K // 512)),
                     pl.BlockSpec((1, L, 512), mk(RM_V // 512)),
                     pl.BlockSpec((1, 512, L), (lambda bi, i: (bi, 0, i)) if mk is fwd
                                  else (lambda bi, i: (bi, 0, nc - 1 - i))),
                     pl.BlockSpec((1, L, LANES), mk(0)),
                     pl.BlockSpec((1, 16, L), (lambda bi, i: (bi, 0, i)) if mk is fwd
                                  else (lambda bi, i: (bi, 0, nc - 1 - i)))]
    in_specs += [_const_spec((1, LANES)), _const_spec((16, ML_CHUNK))]
    out_specs = (pl.BlockSpec((1, L, 512), lambda bi, i: (bi, i, 0)),
                 pl.BlockSpec((1, L, 512), lambda bi, i: (bi, nc - 1 - i, 0)))
    out_shape = (jax.ShapeDtypeStruct((b, s, 512), BF16), jax.ShapeDtypeStruct((b, s, 512), BF16))
    return pl.pallas_call(
        _mlstm_kernel, grid=(b, nc), in_specs=in_specs, out_specs=out_specs, out_shape=out_shape,
        scratch_shapes=[pltpu.VMEM((2 * N_HEADS, LANES, LANES), F32),
                        pltpu.VMEM((2 * N_HEADS, SUBLANES, LANES), F32)],
        compiler_params=_params(2), name="mlstm",
    )(rm, rm, rm, mkt, gates, gates_t, rm, rm, rm, mkt, gates, gates_t, bias_row, bias_col)


def _outproj_kernel(hf_ref, hb_ref, o_ref, pm_ref, pp_ref, pn_ref, yc_ref, yd_ref, x_ref,
                    gml_ref, wpool_ref, pscale_ref, wa_ref, wb_ref, wc_ref, wd_ref, out_ref, *, tm, seq):
    i = pl.program_id(1)
    n_t = pl.num_programs(1)

    hs = hf_ref[0].astype(F32) + hb_ref[0].astype(F32)
    og = o_ref[0].astype(F32)
    parts = []
    for h in range(N_HEADS):
        sl = slice(h * LANES, (h + 1) * LANES)
        hh = hs[:, sl]
        ms = jnp.sum(hh * hh, axis=-1, keepdims=True) * (1.0 / HEAD_DIM)
        gate = 1.0 / (1.0 + jnp.exp(-og[:, sl]))
        parts.append(hh * lax.rsqrt(ms + EPS) * gml_ref[:, sl] * gate)
    ya = jnp.concatenate(parts, axis=1).astype(BF16)

    prev = jnp.where(i > 0, pp_ref[0].astype(F32), 0.0)
    nxt = jnp.where(i < n_t - 1, pn_ref[0].astype(F32), 0.0)
    u = jnp.concatenate([prev, pm_ref[0].astype(F32), nxt], axis=0)
    n = tm + 2 * SUBLANES
    sh = lambda v, k: pltpu.roll(v, k % n, 0)
    w2 = u + sh(u, 1)
    w4 = sh(w2, 1) + sh(w2, -1)
    w8 = sh(w4, 2) + sh(w4, -2)
    w16 = sh(w8, 4) + sh(w8, -4)
    core = slice(SUBLANES, SUBLANES + tm)
    lane = lax.broadcasted_iota(jnp.int32, (1, GROUP_WIDTH), 1)
    t = (i * tm + lax.broadcasted_iota(jnp.int32, (tm, 1), 0)).astype(F32)
    total = jnp.where(lane < 64, w2[core], jnp.where(lane < 128, w4[core],
                      jnp.where(lane < 192, w8[core], w16[core])))
    half = jnp.where(lane < 64, 1.0, jnp.where(lane < 128, 2.0, jnp.where(lane < 192, 4.0, 8.0)))
    lo = jnp.maximum(t - half, 0.0)
    hi = jnp.minimum(t + half - 1.0, seq - 1.0)
    pooled = total / (hi - lo + 1.0) - u[core]
    yb = (_dot(pooled.astype(BF16), wpool_ref[...]) * pscale_ref[...]).astype(BF16)

    out_ref[0] = (x_ref[0] + _dot(ya, wa_ref[...]) + _dot(yb, wb_ref[...])
                  + _dot(yc_ref[0], wc_ref[...]) + _dot(yd_ref[0], wd_ref[...]))


def _out_projection(hf, hb, rm, yc, yd, x, ml_norm, pool_w, pool_scale, w_out, tm=512):
    b, s, d = x.shape
    wa = w_out[0:256].reshape(N_HEADS, HEAD_DIM, d)
    wa = jnp.concatenate([wa, jnp.zeros_like(wa)], axis=1).reshape(N_HEADS * LANES, d).astype(BF16)
    wb, wc, wd = (w_out[256 * j:256 * (j + 1)].astype(BF16) for j in (1, 2, 3))
    gml = ml_norm.reshape(N_HEADS, HEAD_DIM)
    gml = jnp.concatenate([gml, jnp.zeros_like(gml)], axis=1).reshape(1, N_HEADS * LANES).astype(F32)
    wpool = jax.scipy.linalg.block_diag(*[pool_w[g] for g in range(4)]).astype(BF16)
    nb8 = tm // SUBLANES
    last8 = s // SUBLANES - 1
    row = lambda w, cb=0: pl.BlockSpec((1, tm, w), lambda bi, i: (bi, i, cb))
    in_specs = [
        row(512), row(512), row(512, RM_O // 512),
        row(256, RM_POOL // 256),
        pl.BlockSpec((1, SUBLANES, 256), lambda bi, i: (bi, jnp.maximum(i * nb8 - 1, 0), RM_POOL // 256)),
        pl.BlockSpec((1, SUBLANES, 256), lambda bi, i: (bi, jnp.minimum((i + 1) * nb8, last8), RM_POOL // 256)),
        row(256), row(256), row(d),
        _const_spec((1, 512)), _const_spec((256, 256)), _const_spec((1, 256)),
        _const_spec((512, d)), _const_spec((256, d)), _const_spec((256, d)), _const_spec((256, d)),
    ]
    kern = functools.partial(_outproj_kernel, tm=tm, seq=s)
    return pl.pallas_call(
        kern, grid=(b, s // tm), in_specs=in_specs, out_specs=row(d),
        out_shape=jax.ShapeDtypeStruct((b, s, d), F32),
        compiler_params=_params(2), name="out_projection",
    )(hf, hb, rm, rm, rm, rm, yc, yd, x, gml, wpool, pool_scale[None].astype(F32), wa, wb, wc, wd)


def _memkv_kernel(mem_ref, g_ref, w_ref, gk_ref, k_ref, v_ref):
    mn = _rms(mem_ref[0], g_ref[...]).astype(BF16)
    kv = _dot(mn, w_ref[...])
    ks = []
    for h in range(CROSS_HEADS):
        ks.append(_rms(kv[:, h * CROSS_HEAD_DIM:(h + 1) * CROSS_HEAD_DIM], gk_ref[...]))
    k_ref[0] = jnp.concatenate(ks, axis=1).astype(BF16)
    v_ref[0] = kv[:, D_MODEL:].astype(BF16)


def _mem_kv(mem, g_mem, w_kv, g_k):
    b, m, d = mem.shape
    blk = pl.BlockSpec((1, m, d), lambda bi: (bi, 0, 0))
    return pl.pallas_call(
        _memkv_kernel, grid=(b,),
        in_specs=[blk, _const_spec((1, d)), _const_spec((d, 2 * d)), _const_spec((1, CROSS_HEAD_DIM))],
        out_specs=(blk, blk),
        out_shape=(jax.ShapeDtypeStruct((b, m, d), BF16), jax.ShapeDtypeStruct((b, m, d), BF16)),
        compiler_params=_params(1), name="mem_kv",
    )(mem, g_mem[None].astype(F32), w_kv.astype(BF16), g_k[None].astype(F32))


def _cross_kernel(x_ref, g_ref, wq_ref, gq_ref, k_ref, v_ref, wo_ref, out_ref):
    x = x_ref[0]
    xn = _rms(x, g_ref[...]).astype(BF16)
    q = _dot(xn, wq_ref[...])
    outs = []
    for h in range(CROSS_HEADS):
        sl = slice(h * CROSS_HEAD_DIM, (h + 1) * CROSS_HEAD_DIM)
        qh = (_rms(q[:, sl], gq_ref[...]) * (CROSS_HEAD_DIM ** -0.5)).astype(BF16)
        s = _dot_nt(qh, k_ref[0, :, sl])
        p = jnp.exp(s - jnp.max(s, axis=-1, keepdims=True))
        l = jnp.sum(p, axis=-1, keepdims=True)
        outs.append((_dot(p.astype(BF16), v_ref[0, :, sl]) / l).astype(BF16))
    o = jnp.concatenate(outs, axis=1)
    out_ref[0] = x + _dot(o, wo_ref[...])


def _cross_attention(x, g, w_q, g_q, k, v, w_o, tm=512):
    b, s, d = x.shape
    m = k.shape[1]
    row = pl.BlockSpec((1, tm, d), lambda bi, i: (bi, i, 0))
    kvb = pl.BlockSpec((1, m, d), lambda bi, i: (bi, 0, 0))
    return pl.pallas_call(
        _cross_kernel, grid=(b, s // tm),
        in_specs=[row, _const_spec((1, d)), _const_spec((d, d)), _const_spec((1, CROSS_HEAD_DIM)),
                  kvb, kvb, _const_spec((d, d))],
        out_specs=row, out_shape=jax.ShapeDtypeStruct((b, s, d), F32),
        compiler_params=_params(2), name="cross_attention",
    )(x, g[None].astype(F32), w_q.astype(BF16), g_q[None].astype(F32), k, v, w_o.astype(BF16))


FF_CHUNK = 256


def _ffn_kernel(xm_ref, xp_ref, xn_ref, g_ref, win_ref, cw_ref, cb_ref, wout_ref, out_ref, *, tm):
    i = pl.program_id(1)
    n_t = pl.num_programs(1)
    xm = xm_ref[0]
    xe = jnp.concatenate([xp_ref[0], xm, xn_ref[0]], axis=0)
    hn = _rms(xe, g_ref[...]).astype(BF16)
    n = tm + 2 * SUBLANES
    core = slice(SUBLANES, SUBLANES + tm)
    hc = hn[core]
    r = lax.broadcasted_iota(jnp.int32, (n, 1), 0)
    live = jnp.logical_and(jnp.logical_or(i > 0, r >= SUBLANES),
                           jnp.logical_or(i < n_t - 1, r < SUBLANES + tm)).astype(F32)
    acc = xm
    for c in range(D_FF // FF_CHUNK):
        gs = slice(c * FF_CHUNK, (c + 1) * FF_CHUNK)
        us = slice(D_FF + c * FF_CHUNK, D_FF + (c + 1) * FF_CHUNK)
        gate = _dot(hn, win_ref[:, gs]) * live
        cw = cw_ref[:, gs]
        gconv = (pltpu.roll(gate, 1, 0)[core] * cw[0:1] + gate[core] * cw[1:2]
                 + pltpu.roll(gate, n - 1, 0)[core] * cw[2:3] + cb_ref[:, gs])
        up = _dot(hc, win_ref[:, us])
        act = (gconv / (1.0 + jnp.exp(-gconv)) * up).astype(BF16)
        acc = acc + _dot(act, wout_ref[gs, :])
    out_ref[0] = acc


def _conv_ffn(x, g, w_in, conv_w, conv_b, w_out, tm=512):
    b, s, d = x.shape
    nb8 = tm // SUBLANES
    last8 = s // SUBLANES - 1
    row = pl.BlockSpec((1, tm, d), lambda bi, i: (bi, i, 0))
    in_specs = [
        row,
        pl.BlockSpec((1, SUBLANES, d), lambda bi, i: (bi, jnp.maximum(i * nb8 - 1, 0), 0)),
        pl.BlockSpec((1, SUBLANES, d), lambda bi, i: (bi, jnp.minimum((i + 1) * nb8, last8), 0)),
        _const_spec((1, d)), _const_spec((d, 2 * D_FF)), _const_spec((3, D_FF)), _const_spec((1, D_FF)),
        _const_spec((D_FF, d)),
    ]
    return pl.pallas_call(
        functools.partial(_ffn_kernel, tm=tm), grid=(b, s // tm), in_specs=in_specs, out_specs=row,
        out_shape=jax.ShapeDtypeStruct((b, s, d), F32),
        compiler_params=_params(2), name="conv_ffn",
    )(x, x, x, g[None].astype(F32), w_in.astype(BF16), conv_w.astype(F32), conv_b[None].astype(F32),
      w_out.astype(BF16))


def _rope_tables(s):
    pos = jnp.arange(s)
    row = pos // GRID_W
    col = pos - row * GRID_W
    inv = ROPE_THETA ** (-jnp.arange(0, DIFF_QK_DIM, 2, dtype=F32) / DIFF_QK_DIM)
    out = []
    for p in (pos, row, col):
        ang = inv[:, None] * p.astype(F32)[None, :]
        out += [jnp.cos(ang), jnp.sin(ang)]
    return jnp.stack(out)


def kernel(x, mem, norm_mix, w_in, ml_bias_i, ml_bias_f, ml_norm, pool_w, pool_scale, diff_qnorm, diff_knorm, diff_lambda, diff_subnorm, gqa_qnorm, gqa_knorm, w_out, norm_cross, norm_mem, w_cq, w_ckv, cross_qnorm, cross_knorm, w_co, norm_ffn, w_ffn_in, ffn_conv, ffn_conv_b, w_ffn_out):
    depth = w_in.shape[0]
    tabs = _rope_tables(x.shape[1])
    for l in range(depth):
        lam_init = 0.8 - 0.6 * math.exp(-0.3 * l)
        w_rm, w_gate, w_t = _inproj_weights(w_in[l])
        (rm, gates, gates_t, qtd, kd, vtd, qtg, kg, vtg, mkt) = _in_projection(
            x, norm_mix[l], w_rm, w_gate, w_t, tabs, diff_qnorm[l] / _gain_max(diff_qnorm[l]),
            diff_knorm[l] / _gain_max(diff_knorm[l]), gqa_qnorm[l] / _gain_max(gqa_qnorm[l]),
            gqa_knorm[l] / _gain_max(gqa_knorm[l]))
        hf, hb = _mlstm(rm, mkt, gates, gates_t, ml_bias_i[l], ml_bias_f[l])
        yc = _diff_attention(qtd, kd, vtd, diff_lambda[l], diff_subnorm[l], lam_init,
                             _score_bound(diff_qnorm[l], diff_knorm[l], DIFF_QK_DIM),
                             _post_scale(diff_qnorm[l], diff_knorm[l], DIFF_QK_DIM, FP8_SCALE_DIFF))
        yd = _gqa_attention(qtg, kg, vtg, _score_bound(gqa_qnorm[l], gqa_knorm[l], HEAD_DIM),
                            _post_scale(gqa_qnorm[l], gqa_knorm[l], HEAD_DIM, FP8_SCALE_GQA))
        x = _out_projection(hf, hb, rm, yc, yd, x, ml_norm[l], pool_w[l], pool_scale[l], w_out[l])
        ck, cv = _mem_kv(mem, norm_mem[l], w_ckv[l], cross_knorm[l])
        x = _cross_attention(x, norm_cross[l], w_cq[l], cross_qnorm[l], ck, cv, w_co[l])
        x = _conv_ffn(x, norm_ffn[l], w_ffn_in[l], ffn_conv[l], ffn_conv_b[l], w_ffn_out[l])
    return x
```

```python
import functools
import math

import jax
import jax.numpy as jnp
from jax import lax
from jax.experimental import pallas as pl
from jax.experimental.pallas import tpu as pltpu

F32 = jnp.float32
BF16 = jnp.bfloat16
FP8 = jnp.float8_e4m3fn

D_MODEL = 1024
HEAD_DIM = 64
GROUP_WIDTH = 256
N_HEADS = 4
POOL_WINDOWS = (2, 4, 8, 16)
DIFF_QK_DIM = 32
GQA_KV_HEADS = 2
GRID_W = 64
ROPE_THETA = 10000.0
CROSS_HEADS = 4
CROSS_HEAD_DIM = 256
D_FF = 2816
EPS = 1e-6

LANES = 128
SUBLANES = 8
VMEM_LIMIT_BYTES = 56 * 1024 * 1024

V_AUG = HEAD_DIM + 16
ML_CHUNK = 128
ML_SUB = 4
NEG_BIG = -1e30
LOG2E = math.log2(math.e)
MAX_SCORE_BOUND = 50.0

_NT = (((1,), (1,)), ((), ()))


def _dot(a, b):
    return jnp.dot(a, b, preferred_element_type=F32)


def _dot_nt(a, b):
    return lax.dot_general(a, b, _NT, preferred_element_type=F32)


def _dot_f32(a, b):
    return jnp.dot(a, b, preferred_element_type=F32, precision=lax.Precision.HIGHEST)


def _params(n_axes):
    return pltpu.CompilerParams(dimension_semantics=("arbitrary",) * n_axes,
                                vmem_limit_bytes=VMEM_LIMIT_BYTES)


def _const_spec(shape):
    nd = len(shape)
    return pl.BlockSpec(shape, lambda *_: (0,) * nd, pipeline_mode=pl.Buffered(1))


def _rms(x, g):
    ms = jnp.mean(x * x, axis=-1, keepdims=True)
    return x * lax.rsqrt(ms + EPS) * g


RM_Q, RM_K, RM_V, RM_O, RM_POOL = 0, 512, 1024, 1536, 2048
RM_COLS = 2304
T_DQ, T_DK, T_GQ, T_GK, T_DV, T_GV, T_MK, T_GATE = 0, 256, 512, 768, 896, 1152, 1280, 1536
T_ROWS = 1552


def _norm_rope_t(z, norm_rows, gain, cos64, sin64, scale):
    r, t = z.shape
    zg = z.reshape(r // norm_rows, norm_rows, t)
    ms = jnp.mean(zg * zg, axis=1, keepdims=True)
    zn = (zg * lax.rsqrt(ms + EPS) * gain[None]).reshape(r // 64, 64, t)
    sw = jnp.concatenate([zn[:, 16:32], zn[:, 0:16], zn[:, 48:64], zn[:, 32:48]], axis=1)
    out = zn * cos64[None] + sw * sin64[None]
    return (out * scale).reshape(r, t)


FP8_SCALE_DIFF = 64.0
FP8_SCALE_GQA = 32.0


def _split_fp8(z, dim, is_query):
    r, t = z.shape
    z3 = z.reshape(r // dim, dim, t)
    hi = z3.astype(FP8).astype(F32)
    lo = (z3 - hi).astype(FP8).astype(F32)
    parts = [hi, hi, lo] if is_query else [hi, lo, hi]
    return jnp.concatenate(parts + [jnp.zeros_like(hi)], axis=1).reshape(4 * r, t)


def _with_ones_rows(vt, heads):
    t = vt.shape[1]
    v3 = vt.reshape(heads, HEAD_DIM, t)
    ones = jnp.ones((heads, V_AUG - HEAD_DIM, t), F32)
    return jnp.concatenate([v3, ones], axis=1).reshape(heads * V_AUG, t)


def _pad_heads_lanes(z):
    low = lax.broadcasted_iota(jnp.int32, (1, LANES), 1) < HEAD_DIM
    tiles = []
    for j in range(z.shape[1] // LANES):
        tile = z[:, j * LANES:(j + 1) * LANES]
        tiles += [jnp.where(low, tile, 0.0), jnp.where(low, pltpu.roll(tile, HEAD_DIM, 1), 0.0)]
    return jnp.concatenate(tiles, axis=1)


def _pad_heads_rows(zt):
    t = zt.shape[1]
    z3 = zt.reshape(N_HEADS, HEAD_DIM, t)
    return jnp.concatenate([z3, jnp.zeros_like(z3)], axis=1).reshape(N_HEADS * LANES, t)


def _inproj_kernel(x_ref, g_ref, wrm_ref, wg_ref, wt_ref, cscale_ref, cadd_ref, tabs_ref,
                   gqd_ref, gkd_ref, gqg_ref, gkg_ref,
                   rm_ref, gates_ref, gates_t_ref, qtd_ref, kd_ref, vtd_ref,
                   qtg_ref, kg_ref, vtg_ref, mkt_ref):
    x = x_ref[0]
    h = _rms(x, g_ref[...]).astype(BF16)
    zr = _dot(h, wrm_ref[...])
    padded = [_pad_heads_lanes(zr[:, j * GROUP_WIDTH:(j + 1) * GROUP_WIDTH]) for j in range(4)]
    slab = jnp.concatenate(padded + [zr[:, 4 * GROUP_WIDTH:]], axis=1)
    rm_ref[0] = (slab * cscale_ref[...] + cadd_ref[...]).astype(BF16)
    gates_ref[0] = _dot(h, wg_ref[...])
    zt = _dot_nt(wt_ref[...], h)

    tabs = tabs_ref[...]
    c1, s1, rc, rs, cc, cs = (tabs[i] for i in range(6))
    cos_d = jnp.concatenate([c1, c1, c1, c1], axis=0)
    sin_d = jnp.concatenate([-s1, s1, -s1, s1], axis=0)
    cos_g = jnp.concatenate([rc, rc, cc, cc], axis=0)
    sin_g = jnp.concatenate([-rs, rs, -cs, cs], axis=0)

    qd = _norm_rope_t(zt[T_DQ:T_DQ + 256], DIFF_QK_DIM, gqd_ref[...], cos_d, sin_d, FP8_SCALE_DIFF)
    qtd_ref[0] = _split_fp8(qd, DIFF_QK_DIM, True).astype(FP8)
    kd = _norm_rope_t(zt[T_DK:T_DK + 256], DIFF_QK_DIM, gkd_ref[...], cos_d, sin_d, FP8_SCALE_DIFF)
    kd_ref[0] = _split_fp8(kd, DIFF_QK_DIM, False).T.astype(FP8)
    qg = _norm_rope_t(zt[T_GQ:T_GQ + 256], HEAD_DIM, gqg_ref[...], cos_g, sin_g, FP8_SCALE_GQA)
    qtg_ref[0] = _split_fp8(qg, HEAD_DIM, True).astype(FP8)
    kg = _norm_rope_t(zt[T_GK:T_GK + 128], HEAD_DIM, gkg_ref[...], cos_g, sin_g, FP8_SCALE_GQA)
    kg_ref[0] = _split_fp8(kg, HEAD_DIM, False).T.astype(FP8)
    vtd_ref[0] = _with_ones_rows(zt[T_DV:T_DV + 256], N_HEADS).astype(BF16)
    vtg_ref[0] = _with_ones_rows(zt[T_GV:T_GV + 128], GQA_KV_HEADS).astype(BF16)
    mkt_ref[0] = (_pad_heads_rows(zt[T_MK:T_MK + 256]) * (HEAD_DIM ** -0.5)).astype(BF16)
    gates_t_ref[0] = zt[T_GATE:T_GATE + 16]


def _inproj_weights(w_in):
    sizes = (256, 256, 256, 256, 16, 256, 256, 256, 256, 256, 128, 128)
    offs = [0]
    for s in sizes:
        offs.append(offs[-1] + s)
    (ml_q, ml_k, ml_v, ml_o, ml_g, pool, d_q, d_k, d_v, g_q, g_k, g_v) = (
        w_in[:, offs[i]:offs[i + 1]] for i in range(12))
    w_rm = jnp.concatenate([ml_q, ml_k, ml_v, ml_o, pool], axis=1).astype(BF16)
    w_gate = jnp.concatenate([ml_g, jnp.zeros((D_MODEL, LANES - 16), F32)], axis=1).astype(BF16)
    w_t = jnp.concatenate([d_q, d_k, g_q, g_k, d_v, g_v, ml_k, ml_g], axis=1).T.astype(BF16)
    return w_rm, w_gate, w_t


def _inproj_col_consts():
    lane = jnp.arange(RM_COLS)
    in_k = (lane >= RM_K) & (lane < RM_V)
    cscale = jnp.where(in_k, HEAD_DIM ** -0.5, 1.0).astype(F32)[None]
    ones_col = (lane >= RM_V) & (lane < RM_O) & ((lane % LANES) == HEAD_DIM)
    cadd = jnp.where(ones_col, 1.0, 0.0).astype(F32)[None]
    return cscale, cadd


def _in_projection(x, g, w_rm, w_gate, w_t, tabs, gqd, gkd, gqg, gkg, tm=512):
    b, s, d = x.shape
    cscale, cadd = _inproj_col_consts()
    bc = lambda v: jnp.broadcast_to(v.astype(F32)[:, None], (v.shape[0], tm))
    grid = (b, s // tm)
    row_blk = lambda w: pl.BlockSpec((1, tm, w), lambda bi, i: (bi, i, 0))
    col_blk = lambda r: pl.BlockSpec((1, r, tm), lambda bi, i: (bi, 0, i))
    out_shapes = (
        jax.ShapeDtypeStruct((b, s, RM_COLS), BF16),
        jax.ShapeDtypeStruct((b, s, LANES), F32),
        jax.ShapeDtypeStruct((b, 16, s), F32),
        jax.ShapeDtypeStruct((b, 1024, s), FP8),
        jax.ShapeDtypeStruct((b, s, 1024), FP8),
        jax.ShapeDtypeStruct((b, N_HEADS * V_AUG, s), BF16),
        jax.ShapeDtypeStruct((b, 1024, s), FP8),
        jax.ShapeDtypeStruct((b, s, 512), FP8),
        jax.ShapeDtypeStruct((b, GQA_KV_HEADS * V_AUG, s), BF16),
        jax.ShapeDtypeStruct((b, 512, s), BF16),
    )
    out_specs = (row_blk(RM_COLS), row_blk(LANES), col_blk(16), col_blk(1024), row_blk(1024),
                 col_blk(N_HEADS * V_AUG), col_blk(1024), row_blk(512), col_blk(GQA_KV_HEADS * V_AUG),
                 col_blk(512))
    in_specs = [
        row_blk(d),
        _const_spec((1, d)),
        _const_spec(w_rm.shape), _const_spec(w_gate.shape), _const_spec(w_t.shape),
        _const_spec((1, RM_COLS)), _const_spec((1, RM_COLS)),
        pl.BlockSpec((6, 16, tm), lambda bi, i: (0, 0, i)),
        _const_spec((DIFF_QK_DIM, tm)), _const_spec((DIFF_QK_DIM, tm)),
        _const_spec((HEAD_DIM, tm)), _const_spec((HEAD_DIM, tm)),
    ]
    return pl.pallas_call(
        _inproj_kernel, grid=grid, in_specs=in_specs, out_specs=out_specs, out_shape=out_shapes,
        compiler_params=_params(2), name="in_projection",
    )(x, g[None].astype(F32), w_rm, w_gate, w_t, cscale, cadd, tabs, bc(gqd), bc(gkd), bc(gqg), bc(gkg))


def _attn_kernel(*refs, combos, width, tq, tk, n_extra, finalize, bounded):
    post_ref, qt_ref, k_ref, vt_ref = refs[:4]
    extra = refs[4:4 + n_extra]
    o_ref = refs[4 + n_extra]
    m_sc, acc_sc, p_sc = refs[5 + n_extra:]
    n_kv = k_ref.shape[1] // tk
    n_c = len(combos)
    post = post_ref[0, 0]

    m_sc[...] = jnp.full(m_sc.shape, NEG_BIG, F32)
    acc_sc[...] = jnp.zeros(acc_sc.shape, F32)

    def scores(c, start):
        q_row0, k_lane0 = combos[c][0], combos[c][1]
        st = _dot(k_ref[0, pl.ds(start, tk), k_lane0:k_lane0 + width], qt_ref[0, q_row0:q_row0 + width, :])
        return st * post

    def values(c, start):
        v_row0 = combos[c][2]
        return vt_ref[0, v_row0:v_row0 + V_AUG, pl.ds(start, tk)]

    def qk_exp(c, start, slot):
        p_sc[slot, c] = jnp.exp2(scores(c, start).astype(BF16))

    def pv(c, start, slot):
        acc_sc[c] += _dot(values(c, start), p_sc[slot, c])

    def bounded_pair(i, carry):
        s0 = pl.multiple_of(2 * i * tk, tk)
        s1 = pl.multiple_of((2 * i + 1) * tk, tk)
        s2 = pl.multiple_of((2 * i + 2) * tk, tk)
        for c in range(n_c):
            qk_exp(c, s1, 1)
            pv(c, s0, 0)
        for c in range(n_c):
            qk_exp(c, s2, 0)
            pv(c, s1, 1)
        return carry

    def online_body(j, carry):
        start = pl.multiple_of(j * tk, tk)
        for c in range(n_c):
            st = scores(c, start)
            m_prev = m_sc[c]
            m_new = jnp.maximum(m_prev, jnp.max(st, axis=0, keepdims=True))
            p = jnp.exp2(st - m_new).astype(BF16)
            acc_sc[c] = jnp.exp2(m_prev - m_new) * acc_sc[c] + _dot(values(c, start), p)
            m_sc[c] = m_new
        return carry

    if bounded:
        for c in range(n_c):
            qk_exp(c, 0, 0)
        lax.fori_loop(0, n_kv // 2 - 1, bounded_pair, 0)
        for c in range(n_c):
            qk_exp(c, (n_kv - 1) * tk, 1)
            pv(c, (n_kv - 2) * tk, 0)
        for c in range(n_c):
            pv(c, (n_kv - 1) * tk, 1)
    else:
        lax.fori_loop(0, n_kv, online_body, 0)
    out_t = finalize([acc_sc[c] for c in range(len(combos))], extra)
    o_ref[0] = out_t.T.astype(BF16)


def _finalize_diff(accs, extra, *, lam_init):
    lam_ref, gsub_ref = extra
    lp = lam_ref[...]
    lam = (jnp.exp(jnp.sum(lp[0:1] * lp[1:2], axis=1, keepdims=True))
           - jnp.exp(jnp.sum(lp[2:3] * lp[3:4], axis=1, keepdims=True)) + lam_init)
    outs = []
    for h in range(N_HEADS):
        a1, a2 = accs[2 * h], accs[2 * h + 1]
        o = (a1[:HEAD_DIM] / a1[HEAD_DIM:HEAD_DIM + 1]
             - lam * (a2[:HEAD_DIM] / a2[HEAD_DIM:HEAD_DIM + 1]))
        ms = jnp.mean(o * o, axis=0, keepdims=True)
        outs.append(o * lax.rsqrt(ms + EPS) * gsub_ref[...] * (1.0 - lam_init))
    return jnp.concatenate(outs, axis=0)


def _finalize_gqa(accs, extra):
    return jnp.concatenate([a[:HEAD_DIM] / a[HEAD_DIM:HEAD_DIM + 1] for a in accs], axis=0)


def _gain_max(g):
    return jnp.maximum(jnp.max(jnp.abs(g)), 1e-30)


def _post_scale(g_q, g_k, dim, fp8_scale):
    return _gain_max(g_q) * _gain_max(g_k) * (dim ** -0.5) * LOG2E / (fp8_scale * fp8_scale)


def _score_bound(g_q, g_k, dim):
    return 1.02 * (dim ** 0.5) * LOG2E * jnp.max(jnp.abs(g_q)) * jnp.max(jnp.abs(g_k))


def _attention(qt, k, vt, extra, bound, post, *, name, combos, width, finalize, tq=512, tk=256):
    b, _, s = qt.shape
    n_c = len(combos)
    in_specs = [
        pl.BlockSpec(memory_space=pltpu.SMEM),
        pl.BlockSpec((1, qt.shape[1], tq), lambda bi, i: (bi, 0, i)),
        pl.BlockSpec((1, s, k.shape[2]), lambda bi, i: (bi, 0, 0)),
        pl.BlockSpec((1, vt.shape[1], s), lambda bi, i: (bi, 0, 0)),
    ] + [_const_spec(e.shape) for e in extra]

    def call(bounded):
        kern = functools.partial(_attn_kernel, combos=combos, width=width, tq=tq, tk=tk,
                                 n_extra=len(extra), finalize=finalize, bounded=bounded)
        return pl.pallas_call(
            kern, grid=(b, s // tq), in_specs=in_specs,
            out_specs=pl.BlockSpec((1, tq, 256), lambda bi, i: (bi, i, 0)),
            out_shape=jax.ShapeDtypeStruct((b, s, 256), BF16),
            scratch_shapes=[pltpu.VMEM((n_c, 1, tq), F32),
                            pltpu.VMEM((n_c, V_AUG, tq), F32),
                            pltpu.VMEM((2, n_c, tk, tq) if bounded else (1, 1, 16, LANES), BF16)],
            compiler_params=_params(2),
            name="attention_%s_%s" % (name, "bounded" if bounded else "online"),
        )(post.reshape(1, 1).astype(F32), qt, k, vt, *extra)

    return lax.cond(bound <= MAX_SCORE_BOUND, lambda: call(True), lambda: call(False))


def _diff_attention(qt, k, vt, lam_params, g_sub, lam_init, bound, post, tq=512):
    combos = tuple((128 * (2 * h + c), 128 * (2 * h + c), V_AUG * h)
                   for h in range(N_HEADS) for c in range(2))
    gsub_b = jnp.broadcast_to(g_sub.astype(F32)[:, None], (HEAD_DIM, tq))
    fin = functools.partial(_finalize_diff, lam_init=lam_init)
    return _attention(qt, k, vt, (lam_params.astype(F32), gsub_b), bound, post, name="diff", combos=combos,
                      width=4 * DIFF_QK_DIM, finalize=fin, tq=tq)


def _gqa_attention(qt, k, vt, bound, post, tq=512):
    combos = tuple((256 * qh, 256 * (qh // 2), V_AUG * (qh // 2)) for qh in range(N_HEADS))
    return _attention(qt, k, vt, (), bound, post, name="gqa", combos=combos, width=4 * HEAD_DIM,
                      finalize=_finalize_gqa, tq=tq)


def _log_sigmoid(x):
    return jnp.minimum(x, 0.0) - jnp.log(1.0 + jnp.exp(-jnp.abs(x)))


def _mlstm_kernel(qf_ref, kf_ref, vf_ref, ktf_ref, gf_ref, gtf_ref,
                  qb_ref, kb_ref, vb_ref, ktb_ref, gb_ref, gtb_ref,
                  brow_ref, bcol_ref, hf_ref, hb_ref, c_sc, m_sc):
    L = ML_CHUNK

    @pl.when(pl.program_id(1) == 0)
    def _():
        c_sc[...] = jnp.zeros(c_sc.shape, F32)
        m_sc[...] = jnp.zeros(m_sc.shape, F32)

    row = lax.broadcasted_iota(jnp.int32, (L, L), 0)
    col = lax.broadcasted_iota(jnp.int32, (L, L), 1)
    lower = col <= row
    upper = col >= row
    lower_f = lower.astype(F32)
    upper_f = upper.astype(F32)
    lane = lax.broadcasted_iota(jnp.int32, (1, LANES), 1)
    forget_lane = (lane % 8) >= 4
    grow = lax.broadcasted_iota(jnp.int32, (16, 1), 0)
    forget_row = (grow % 8) >= 4
    keep = (lax.broadcasted_iota(jnp.int32, (1, LANES), 1) < HEAD_DIM).astype(F32)

    fwd = (0, qf_ref, kf_ref, vf_ref, ktf_ref, gf_ref, gtf_ref, hf_ref, lower, lower_f, upper_f, L - 1)
    bwd = (1, qb_ref, kb_ref, vb_ref, ktb_ref, gb_ref, gtb_ref, hb_ref, upper, upper_f, lower_f, 0)
    work = []
    for j in range(ML_SUB):
        work += [fwd + (j * L,), bwd + ((ML_SUB - 1 - j) * L,)]
    for d, q_ref, k_ref, v_ref, kt_ref, g_ref, gt_ref, h_ref, mask, tri_c, tri_r, last, r0 in work:
        rows = slice(r0, r0 + L)
        g = g_ref[0, rows, :] + brow_ref[...]
        gp = jnp.where(forget_lane, _log_sigmoid(g), g)
        gt = gt_ref[0, :, rows] + bcol_ref[...]
        gtp = jnp.where(forget_row, _log_sigmoid(gt), gt)
        bcol_all = _dot_f32(tri_c, gp)
        brow_all = _dot_f32(gtp, tri_r)
        for h in range(N_HEADS):
            ii, fi = d * 8 + h, d * 8 + 4 + h
            idx = d * N_HEADS + h
            b_col = bcol_all[:, fi:fi + 1]
            b_row = brow_all[fi:fi + 1, :]
            c_row = gtp[ii:ii + 1, :] - b_row
            g_tot = b_row[:, last:last + 1]
            m_prev = m_sc[idx][0:1, 0:1]

            qh = q_ref[0, rows, h * LANES:(h + 1) * LANES]
            kh = k_ref[0, rows, h * LANES:(h + 1) * LANES]
            vh = v_ref[0, rows, h * LANES:(h + 1) * LANES]
            kth = kt_ref[0, h * LANES:(h + 1) * LANES, rows]

            cmask = jnp.where(mask, c_row, -jnp.inf)
            mu = jnp.maximum(m_prev, jnp.max(cmask, axis=1, keepdims=True))
            wts = (jnp.exp(cmask - mu) * _dot_nt(qh, kh)).astype(BF16)
            s_inter = jnp.exp(m_prev - mu)
            c_prev = c_sc[idx]
            tot = _dot(wts, vh) + s_inter * _dot(qh, c_prev.astype(BF16))
            den = tot[:, HEAD_DIM:HEAD_DIM + 1]
            hout = tot / jnp.maximum(jnp.abs(den), jnp.exp(-(b_col + mu)))
            h_ref[0, rows, h * LANES:(h + 1) * LANES] = (hout * keep).astype(BF16)

            c_max = jnp.max(c_row, axis=1, keepdims=True)
            m_loc = g_tot + c_max
            w_row = jnp.exp(c_row - c_max)
            c_loc = _dot((kth.astype(F32) * w_row).astype(BF16), vh)
            m_new = jnp.maximum(g_tot + m_prev, m_loc)
            c_sc[idx] = jnp.exp(g_tot + m_prev - m_new) * c_prev + jnp.exp(m_loc - m_new) * c_loc
            m_sc[idx] = jnp.broadcast_to(m_new, (SUBLANES, LANES))


def _mlstm(rm, mkt, gates, gates_t, b_i, b_f):
    b, s, _ = rm.shape
    L = ML_CHUNK * ML_SUB
    nc = s // L
    bias = jnp.stack([b_i[0], b_f[0], b_i[1], b_f[1]]).reshape(16).astype(F32)
    bias_row = jnp.concatenate([bias, jnp.zeros((LANES - 16,), F32)])[None]
    bias_col = jnp.broadcast_to(bias[:, None], (16, ML_CHUNK))
    fwd = lambda cblk: (lambda bi, i: (bi, i, cblk))
    bwd = lambda cblk: (lambda bi, i: (bi, nc - 1 - i, cblk))
    in_specs = []
    for mk in (fwd, bwd):
        in_specs += [pl.BlockSpec((1, L, 512), mk(RM_Q // 512)),
                     pl.BlockSpec((1, L, 512), mk(RM_K // 512)),
                     pl.BlockSpec((1, L, 512), mk(RM_V // 512)),
                     pl.BlockSpec((1, 512, L), (lambda bi, i: (bi, 0, i)) if mk is fwd
                                  else (lambda bi, i: (bi, 0, nc - 1 - i))),
                     pl.BlockSpec((1, L, LANES), mk(0)),
                     pl.BlockSpec((1, 16, L), (lambda bi, i: (bi, 0, i)) if mk is fwd
                                  else (lambda bi, i: (bi, 0, nc - 1 - i)))]
    in_specs += [_const_spec((1, LANES)), _const_spec((16, ML_CHUNK))]
    out_specs = (pl.BlockSpec((1, L, 512), lambda bi, i: (bi, i, 0)),
                 pl.BlockSpec((1, L, 512), lambda bi, i: (bi, nc - 1 - i, 0)))
    out_shape = (jax.ShapeDtypeStruct((b, s, 512), BF16), jax.ShapeDtypeStruct((b, s, 512), BF16))
    return pl.pallas_call(
        _mlstm_kernel, grid=(b, nc), in_specs=in_specs, out_specs=out_specs, out_shape=out_shape,
        scratch_shapes=[pltpu.VMEM((2 * N_HEADS, LANES, LANES), F32),
                        pltpu.VMEM((2 * N_HEADS, SUBLANES, LANES), F32)],
        compiler_params=_params(2), name="mlstm",
    )(rm, rm, rm, mkt, gates, gates_t, rm, rm, rm, mkt, gates, gates_t, bias_row, bias_col)


def _outproj_kernel(hf_ref, hb_ref, o_ref, pm_ref, pp_ref, pn_ref, yc_ref, yd_ref, x_ref,
                    gml_ref, wpool_ref, pscale_ref, wa_ref, wb_ref, wc_ref, wd_ref, out_ref, *, tm, seq):
    i = pl.program_id(1)
    n_t = pl.num_programs(1)

    hs = hf_ref[0].astype(F32) + hb_ref[0].astype(F32)
    og = o_ref[0].astype(F32)
    parts = []
    for h in range(N_HEADS):
        sl = slice(h * LANES, (h + 1) * LANES)
        hh = hs[:, sl]
        ms = jnp.sum(hh * hh, axis=-1, keepdims=True) * (1.0 / HEAD_DIM)
        gate = 1.0 / (1.0 + jnp.exp(-og[:, sl]))
        parts.append(hh * lax.rsqrt(ms + EPS) * gml_ref[:, sl] * gate)
    ya = jnp.concatenate(parts, axis=1).astype(BF16)

    prev = jnp.where(i > 0, pp_ref[0].astype(F32), 0.0)
    nxt = jnp.where(i < n_t - 1, pn_ref[0].astype(F32), 0.0)
    u = jnp.concatenate([prev, pm_ref[0].astype(F32), nxt], axis=0)
    n = tm + 2 * SUBLANES
    sh = lambda v, k: pltpu.roll(v, k % n, 0)
    w2 = u + sh(u, 1)
    w4 = sh(w2, 1) + sh(w2, -1)
    w8 = sh(w4, 2) + sh(w4, -2)
    w16 = sh(w8, 4) + sh(w8, -4)
    core = slice(SUBLANES, SUBLANES + tm)
    lane = lax.broadcasted_iota(jnp.int32, (1, GROUP_WIDTH), 1)
    t = (i * tm + lax.broadcasted_iota(jnp.int32, (tm, 1), 0)).astype(F32)
    total = jnp.where(lane < 64, w2[core], jnp.where(lane < 128, w4[core],
                      jnp.where(lane < 192, w8[core], w16[core])))
    half = jnp.where(lane < 64, 1.0, jnp.where(lane < 128, 2.0, jnp.where(lane < 192, 4.0, 8.0)))
    lo = jnp.maximum(t - half, 0.0)
    hi = jnp.minimum(t + half - 1.0, seq - 1.0)
    pooled = total / (hi - lo + 1.0) - u[core]
    yb = (_dot(pooled.astype(BF16), wpool_ref[...]) * pscale_ref[...]).astype(BF16)

    out_ref[0] = (x_ref[0] + _dot(ya, wa_ref[...]) + _dot(yb, wb_ref[...])
                  + _dot(yc_ref[0], wc_ref[...]) + _dot(yd_ref[0], wd_ref[...]))


def _out_projection(hf, hb, rm, yc, yd, x, ml_norm, pool_w, pool_scale, w_out, tm=512):
    b, s, d = x.shape
    wa = w_out[0:256].reshape(N_HEADS, HEAD_DIM, d)
    wa = jnp.concatenate([wa, jnp.zeros_like(wa)], axis=1).reshape(N_HEADS * LANES, d).astype(BF16)
    wb, wc, wd = (w_out[256 * j:256 * (j + 1)].astype(BF16) for j in (1, 2, 3))
    gml = ml_norm.reshape(N_HEADS, HEAD_DIM)
    gml = jnp.concatenate([gml, jnp.zeros_like(gml)], axis=1).reshape(1, N_HEADS * LANES).astype(F32)
    wpool = jax.scipy.linalg.block_diag(*[pool_w[g] for g in range(4)]).astype(BF16)
    nb8 = tm // SUBLANES
    last8 = s // SUBLANES - 1
    row = lambda w, cb=0: pl.BlockSpec((1, tm, w), lambda bi, i: (bi, i, cb))
    in_specs = [
        row(512), row(512), row(512, RM_O // 512),
        row(256, RM_POOL // 256),
        pl.BlockSpec((1, SUBLANES, 256), lambda bi, i: (bi, jnp.maximum(i * nb8 - 1, 0), RM_POOL // 256)),
        pl.BlockSpec((1, SUBLANES, 256), lambda bi, i: (bi, jnp.minimum((i + 1) * nb8, last8), RM_POOL // 256)),
        row(256), row(256), row(d),
        _const_spec((1, 512)), _const_spec((256, 256)), _const_spec((1, 256)),
        _const_spec((512, d)), _const_spec((256, d)), _const_spec((256, d)), _const_spec((256, d)),
    ]
    kern = functools.partial(_outproj_kernel, tm=tm, seq=s)
    return pl.pallas_call(
        kern, grid=(b, s // tm), in_specs=in_specs, out_specs=row(d),
        out_shape=jax.ShapeDtypeStruct((b, s, d), F32),
        compiler_params=_params(2), name="out_projection",
    )(hf, hb, rm, rm, rm, rm, yc, yd, x, gml, wpool, pool_scale[None].astype(F32), wa, wb, wc, wd)


def _memkv_kernel(mem_ref, g_ref, w_ref, gk_ref, k_ref, v_ref):
    mn = _rms(mem_ref[0], g_ref[...]).astype(BF16)
    kv = _dot(mn, w_ref[...])
    ks = []
    for h in range(CROSS_HEADS):
        ks.append(_rms(kv[:, h * CROSS_HEAD_DIM:(h + 1) * CROSS_HEAD_DIM], gk_ref[...]))
    k_ref[0] = jnp.concatenate(ks, axis=1).astype(BF16)
    v_ref[0] = kv[:, D_MODEL:].astype(BF16)


def _mem_kv(mem, g_mem, w_kv, g_k):
    b, m, d = mem.shape
    blk = pl.BlockSpec((1, m, d), lambda bi: (bi, 0, 0))
    return pl.pallas_call(
        _memkv_kernel, grid=(b,),
        in_specs=[blk, _const_spec((1, d)), _const_spec((d, 2 * d)), _const_spec((1, CROSS_HEAD_DIM))],
        out_specs=(blk, blk),
        out_shape=(jax.ShapeDtypeStruct((b, m, d), BF16), jax.ShapeDtypeStruct((b, m, d), BF16)),
        compiler_params=_params(1), name="mem_kv",
    )(mem, g_mem[None].astype(F32), w_kv.astype(BF16), g_k[None].astype(F32))


def _cross_kernel(x_ref, g_ref, wq_ref, gq_ref, k_ref, v_ref, wo_ref, out_ref):
    x = x_ref[0]
    xn = _rms(x, g_ref[...]).astype(BF16)
    q = _dot(xn, wq_ref[...])
    outs = []
    for h in range(CROSS_HEADS):
        sl = slice(h * CROSS_HEAD_DIM, (h + 1) * CROSS_HEAD_DIM)
        qh = (_rms(q[:, sl], gq_ref[...]) * (CROSS_HEAD_DIM ** -0.5)).astype(BF16)
        s = _dot_nt(qh, k_ref[0, :, sl])
        p = jnp.exp(s - jnp.max(s, axis=-1, keepdims=True))
        l = jnp.sum(p, axis=-1, keepdims=True)
        outs.append((_dot(p.astype(BF16), v_ref[0, :, sl]) / l).astype(BF16))
    o = jnp.concatenate(outs, axis=1)
    out_ref[0] = x + _dot(o, wo_ref[...])


def _cross_attention(x, g, w_q, g_q, k, v, w_o, tm=512):
    b, s, d = x.shape
    m = k.shape[1]
    row = pl.BlockSpec((1, tm, d), lambda bi, i: (bi, i, 0))
    kvb = pl.BlockSpec((1, m, d), lambda bi, i: (bi, 0, 0))
    return pl.pallas_call(
        _cross_kernel, grid=(b, s // tm),
        in_specs=[row, _const_spec((1, d)), _const_spec((d, d)), _const_spec((1, CROSS_HEAD_DIM)),
                  kvb, kvb, _const_spec((d, d))],
        out_specs=row, out_shape=jax.ShapeDtypeStruct((b, s, d), F32),
        compiler_params=_params(2), name="cross_attention",
    )(x, g[None].astype(F32), w_q.astype(BF16), g_q[None].astype(F32), k, v, w_o.astype(BF16))


FF_CHUNK = 256


def _ffn_kernel(xm_ref, xp_ref, xn_ref, g_ref, win_ref, cw_ref, cb_ref, wout_ref, out_ref, *, tm):
    i = pl.program_id(1)
    n_t = pl.num_programs(1)
    xm = xm_ref[0]
    xe = jnp.concatenate([xp_ref[0], xm, xn_ref[0]], axis=0)
    hn = _rms(xe, g_ref[...]).astype(BF16)
    n = tm + 2 * SUBLANES
    core = slice(SUBLANES, SUBLANES + tm)
    hc = hn[core]
    r = lax.broadcasted_iota(jnp.int32, (n, 1), 0)
    live = jnp.logical_and(jnp.logical_or(i > 0, r >= SUBLANES),
                           jnp.logical_or(i < n_t - 1, r < SUBLANES + tm)).astype(F32)
    acc = xm
    for c in range(D_FF // FF_CHUNK):
        gs = slice(c * FF_CHUNK, (c + 1) * FF_CHUNK)
        us = slice(D_FF + c * FF_CHUNK, D_FF + (c + 1) * FF_CHUNK)
        gate = _dot(hn, win_ref[:, gs]) * live
        cw = cw_ref[:, gs]
        gconv = (pltpu.roll(gate, 1, 0)[core] * cw[0:1] + gate[core] * cw[1:2]
                 + pltpu.roll(gate, n - 1, 0)[core] * cw[2:3] + cb_ref[:, gs])
        up = _dot(hc, win_ref[:, us])
        act = (gconv / (1.0 + jnp.exp(-gconv)) * up).astype(BF16)
        acc = acc + _dot(act, wout_ref[gs, :])
    out_ref[0] = acc


def _conv_ffn(x, g, w_in, conv_w, conv_b, w_out, tm=512):
    b, s, d = x.shape
    nb8 = tm // SUBLANES
    last8 = s // SUBLANES - 1
    row = pl.BlockSpec((1, tm, d), lambda bi, i: (bi, i, 0))
    in_specs = [
        row,
        pl.BlockSpec((1, SUBLANES, d), lambda bi, i: (bi, jnp.maximum(i * nb8 - 1, 0), 0)),
        pl.BlockSpec((1, SUBLANES, d), lambda bi, i: (bi, jnp.minimum((i + 1) * nb8, last8), 0)),
        _const_spec((1, d)), _const_spec((d, 2 * D_FF)), _const_spec((3, D_FF)), _const_spec((1, D_FF)),
        _const_spec((D_FF, d)),
    ]
    return pl.pallas_call(
        functools.partial(_ffn_kernel, tm=tm), grid=(b, s // tm), in_specs=in_specs, out_specs=row,
        out_shape=jax.ShapeDtypeStruct((b, s, d), F32),
        compiler_params=_params(2), name="conv_ffn",
    )(x, x, x, g[None].astype(F32), w_in.astype(BF16), conv_w.astype(F32), conv_b[None].astype(F32),
      w_out.astype(BF16))


def _rope_tables(s):
    pos = jnp.arange(s)
    row = pos // GRID_W
    col = pos - row * GRID_W
    inv = ROPE_THETA ** (-jnp.arange(0, DIFF_QK_DIM, 2, dtype=F32) / DIFF_QK_DIM)
    out = []
    for p in (pos, row, col):
        ang = inv[:, None] * p.astype(F32)[None, :]
        out += [jnp.cos(ang), jnp.sin(ang)]
    return jnp.stack(out)


def kernel(x, mem, norm_mix, w_in, ml_bias_i, ml_bias_f, ml_norm, pool_w, pool_scale, diff_qnorm, diff_knorm, diff_lambda, diff_subnorm, gqa_qnorm, gqa_knorm, w_out, norm_cross, norm_mem, w_cq, w_ckv, cross_qnorm, cross_knorm, w_co, norm_ffn, w_ffn_in, ffn_conv, ffn_conv_b, w_ffn_out):
    depth = w_in.shape[0]
    tabs = _rope_tables(x.shape[1])
    for l in range(depth):
        lam_init = 0.8 - 0.6 * math.exp(-0.3 * l)
        w_rm, w_gate, w_t = _inproj_weights(w_in[l])
        (rm, gates, gates_t, qtd, kd, vtd, qtg, kg, vtg, mkt) = _in_projection(
            x, norm_mix[l], w_rm, w_gate, w_t, tabs, diff_qnorm[l] / _gain_max(diff_qnorm[l]),
            diff_knorm[l] / _gain_max(diff_knorm[l]), gqa_qnorm[l] / _gain_max(gqa_qnorm[l]),
            gqa_knorm[l] / _gain_max(gqa_knorm[l]))
        hf, hb = _mlstm(rm, mkt, gates, gates_t, ml_bias_i[l], ml_bias_f[l])
        yc = _diff_attention(qtd, kd, vtd, diff_lambda[l], diff_subnorm[l], lam_init,
                             _score_bound(diff_qnorm[l], diff_knorm[l], DIFF_QK_DIM),
                             _post_scale(diff_qnorm[l], diff_knorm[l], DIFF_QK_DIM, FP8_SCALE_DIFF))
        yd = _gqa_attention(qtg, kg, vtg, _score_bound(gqa_qnorm[l], gqa_knorm[l], HEAD_DIM),
                            _post_scale(gqa_qnorm[l], gqa_knorm[l], HEAD_DIM, FP8_SCALE_GQA))
        x = _out_projection(hf, hb, rm, yc, yd, x, ml_norm[l], pool_w[l], pool_scale[l], w_out[l])
        ck, cv = _mem_kv(mem, norm_mem[l], w_ckv[l], cross_knorm[l])
        x = _cross_attention(x, norm_cross[l], w_cq[l], cross_qnorm[l], ck, cv, w_co[l])
        x = _conv_ffn(x, norm_ffn[l], w_ffn_in[l], ffn_conv[l], ffn_conv_b[l], w_ffn_out[l])
    return x
```

```python
import functools
import math

import jax
import jax.numpy as jnp
from jax import lax
from jax.experimental import pallas as pl
from jax.experimental.pallas import tpu as pltpu

F32 = jnp.float32
BF16 = jnp.bfloat16
FP8 = jnp.float8_e4m3fn

D_MODEL = 1024
HEAD_DIM = 64
GROUP_WIDTH = 256
N_HEADS = 4
POOL_WINDOWS = (2, 4, 8, 16)
DIFF_QK_DIM = 32
GQA_KV_HEADS = 2
GRID_W = 64
ROPE_THETA = 10000.0
CROSS_HEADS = 4
CROSS_HEAD_DIM = 256
D_FF = 2816
EPS = 1e-6

LANES = 128
SUBLANES = 8
VMEM_LIMIT_BYTES = 56 * 1024 * 1024

V_AUG = HEAD_DIM + 16
ML_CHUNK = 128
ML_SUB = 8
NEG_BIG = -1e30
LOG2E = math.log2(math.e)
MAX_SCORE_BOUND = 50.0

_NT = (((1,), (1,)), ((), ()))


def _dot(a, b):
    return jnp.dot(a, b, preferred_element_type=F32)


def _dot_nt(a, b):
    return lax.dot_general(a, b, _NT, preferred_element_type=F32)


def _dot_f32(a, b):
    return jnp.dot(a, b, preferred_element_type=F32, precision=lax.Precision.HIGHEST)


def _params(n_axes):
    return pltpu.CompilerParams(dimension_semantics=("arbitrary",) * n_axes,
                                vmem_limit_bytes=VMEM_LIMIT_BYTES)


def _const_spec(shape):
    nd = len(shape)
    return pl.BlockSpec(shape, lambda *_: (0,) * nd, pipeline_mode=pl.Buffered(1))


def _rms(x, g):
    ms = jnp.mean(x * x, axis=-1, keepdims=True)
    return x * lax.rsqrt(ms + EPS) * g


RM_Q, RM_K, RM_V, RM_O, RM_POOL = 0, 512, 1024, 1536, 2048
RM_COLS = 2304
T_DQ, T_DK, T_GQ, T_GK, T_DV, T_GV, T_MK, T_GATE = 0, 256, 512, 768, 896, 1152, 1280, 1536
T_ROWS = 1552


def _norm_rope_t(z, norm_rows, gain, cos64, sin64, scale):
    r, t = z.shape
    zg = z.reshape(r // norm_rows, norm_rows, t)
    ms = jnp.mean(zg * zg, axis=1, keepdims=True)
    zn = (zg * lax.rsqrt(ms + EPS) * gain[None]).reshape(r // 64, 64, t)
    sw = jnp.concatenate([zn[:, 16:32], zn[:, 0:16], zn[:, 48:64], zn[:, 32:48]], axis=1)
    out = zn * cos64[None] + sw * sin64[None]
    return (out * scale).reshape(r, t)


FP8_SCALE_DIFF = 64.0
FP8_SCALE_GQA = 32.0


def _split_fp8(z, dim, is_query):
    r, t = z.shape
    z3 = z.reshape(r // dim, dim, t)
    hi = z3.astype(FP8).astype(F32)
    lo = (z3 - hi).astype(FP8).astype(F32)
    parts = [hi, hi, lo] if is_query else [hi, lo, hi]
    return jnp.concatenate(parts + [jnp.zeros_like(hi)], axis=1).reshape(4 * r, t)


def _with_ones_rows(vt, heads):
    t = vt.shape[1]
    v3 = vt.reshape(heads, HEAD_DIM, t)
    ones = jnp.ones((heads, V_AUG - HEAD_DIM, t), F32)
    return jnp.concatenate([v3, ones], axis=1).reshape(heads * V_AUG, t)


def _pad_heads_lanes(z):
    low = lax.broadcasted_iota(jnp.int32, (1, LANES), 1) < HEAD_DIM
    tiles = []
    for j in range(z.shape[1] // LANES):
        tile = z[:, j * LANES:(j + 1) * LANES]
        tiles += [jnp.where(low, tile, 0.0), jnp.where(low, pltpu.roll(tile, HEAD_DIM, 1), 0.0)]
    return jnp.concatenate(tiles, axis=1)


def _pad_heads_rows(zt):
    t = zt.shape[1]
    z3 = zt.reshape(N_HEADS, HEAD_DIM, t)
    return jnp.concatenate([z3, jnp.zeros_like(z3)], axis=1).reshape(N_HEADS * LANES, t)


def _inproj_kernel(x_ref, g_ref, wrm_ref, wg_ref, wt_ref, cscale_ref, cadd_ref, tabs_ref,
                   gqd_ref, gkd_ref, gqg_ref, gkg_ref,
                   rm_ref, gates_ref, gates_t_ref, qtd_ref, kd_ref, vtd_ref,
                   qtg_ref, kg_ref, vtg_ref, mkt_ref):
    x = x_ref[0]
    h = _rms(x, g_ref[...]).astype(BF16)
    zr = _dot(h, wrm_ref[...])
    padded = [_pad_heads_lanes(zr[:, j * GROUP_WIDTH:(j + 1) * GROUP_WIDTH]) for j in range(4)]
    slab = jnp.concatenate(padded + [zr[:, 4 * GROUP_WIDTH:]], axis=1)
    rm_ref[0] = (slab * cscale_ref[...] + cadd_ref[...]).astype(BF16)
    gates_ref[0] = _dot(h, wg_ref[...])
    zt = _dot_nt(wt_ref[...], h)

    tabs = tabs_ref[...]
    c1, s1, rc, rs, cc, cs = (tabs[i] for i in range(6))
    cos_d = jnp.concatenate([c1, c1, c1, c1], axis=0)
    sin_d = jnp.concatenate([-s1, s1, -s1, s1], axis=0)
    cos_g = jnp.concatenate([rc, rc, cc, cc], axis=0)
    sin_g = jnp.concatenate([-rs, rs, -cs, cs], axis=0)

    qd = _norm_rope_t(zt[T_DQ:T_DQ + 256], DIFF_QK_DIM, gqd_ref[...], cos_d, sin_d, FP8_SCALE_DIFF)
    qtd_ref[0] = _split_fp8(qd, DIFF_QK_DIM, True).astype(FP8)
    kd = _norm_rope_t(zt[T_DK:T_DK + 256], DIFF_QK_DIM, gkd_ref[...], cos_d, sin_d, FP8_SCALE_DIFF)
    kd_ref[0] = _split_fp8(kd, DIFF_QK_DIM, False).T.astype(FP8)
    qg = _norm_rope_t(zt[T_GQ:T_GQ + 256], HEAD_DIM, gqg_ref[...], cos_g, sin_g, FP8_SCALE_GQA)
    qtg_ref[0] = _split_fp8(qg, HEAD_DIM, True).astype(FP8)
    kg = _norm_rope_t(zt[T_GK:T_GK + 128], HEAD_DIM, gkg_ref[...], cos_g, sin_g, FP8_SCALE_GQA)
    kg_ref[0] = _split_fp8(kg, HEAD_DIM, False).T.astype(FP8)
    vtd_ref[0] = _with_ones_rows(zt[T_DV:T_DV + 256], N_HEADS).astype(BF16)
    vtg_ref[0] = _with_ones_rows(zt[T_GV:T_GV + 128], GQA_KV_HEADS).astype(BF16)
    mkt_ref[0] = (_pad_heads_rows(zt[T_MK:T_MK + 256]) * (HEAD_DIM ** -0.5)).astype(BF16)
    gates_t_ref[0] = zt[T_GATE:T_GATE + 16]


def _inproj_weights(w_in):
    sizes = (256, 256, 256, 256, 16, 256, 256, 256, 256, 256, 128, 128)
    offs = [0]
    for s in sizes:
        offs.append(offs[-1] + s)
    (ml_q, ml_k, ml_v, ml_o, ml_g, pool, d_q, d_k, d_v, g_q, g_k, g_v) = (
        w_in[:, offs[i]:offs[i + 1]] for i in range(12))
    w_rm = jnp.concatenate([ml_q, ml_k, ml_v, ml_o, pool], axis=1).astype(BF16)
    w_gate = jnp.concatenate([ml_g, jnp.zeros((D_MODEL, LANES - 16), F32)], axis=1).astype(BF16)
    w_t = jnp.concatenate([d_q, d_k, g_q, g_k, d_v, g_v, ml_k, ml_g], axis=1).T.astype(BF16)
    return w_rm, w_gate, w_t


def _inproj_col_consts():
    lane = jnp.arange(RM_COLS)
    in_k = (lane >= RM_K) & (lane < RM_V)
    cscale = jnp.where(in_k, HEAD_DIM ** -0.5, 1.0).astype(F32)[None]
    ones_col = (lane >= RM_V) & (lane < RM_O) & ((lane % LANES) == HEAD_DIM)
    cadd = jnp.where(ones_col, 1.0, 0.0).astype(F32)[None]
    return cscale, cadd


def _in_projection(x, g, w_rm, w_gate, w_t, tabs, gqd, gkd, gqg, gkg, tm=512):
    b, s, d = x.shape
    cscale, cadd = _inproj_col_consts()
    bc = lambda v: jnp.broadcast_to(v.astype(F32)[:, None], (v.shape[0], tm))
    grid = (b, s // tm)
    row_blk = lambda w: pl.BlockSpec((1, tm, w), lambda bi, i: (bi, i, 0))
    col_blk = lambda r: pl.BlockSpec((1, r, tm), lambda bi, i: (bi, 0, i))
    out_shapes = (
        jax.ShapeDtypeStruct((b, s, RM_COLS), BF16),
        jax.ShapeDtypeStruct((b, s, LANES), F32),
        jax.ShapeDtypeStruct((b, 16, s), F32),
        jax.ShapeDtypeStruct((b, 1024, s), FP8),
        jax.ShapeDtypeStruct((b, s, 1024), FP8),
        jax.ShapeDtypeStruct((b, N_HEADS * V_AUG, s), BF16),
        jax.ShapeDtypeStruct((b, 1024, s), FP8),
        jax.ShapeDtypeStruct((b, s, 512), FP8),
        jax.ShapeDtypeStruct((b, GQA_KV_HEADS * V_AUG, s), BF16),
        jax.ShapeDtypeStruct((b, 512, s), BF16),
    )
    out_specs = (row_blk(RM_COLS), row_blk(LANES), col_blk(16), col_blk(1024), row_blk(1024),
                 col_blk(N_HEADS * V_AUG), col_blk(1024), row_blk(512), col_blk(GQA_KV_HEADS * V_AUG),
                 col_blk(512))
    in_specs = [
        row_blk(d),
        _const_spec((1, d)),
        _const_spec(w_rm.shape), _const_spec(w_gate.shape), _const_spec(w_t.shape),
        _const_spec((1, RM_COLS)), _const_spec((1, RM_COLS)),
        pl.BlockSpec((6, 16, tm), lambda bi, i: (0, 0, i)),
        _const_spec((DIFF_QK_DIM, tm)), _const_spec((DIFF_QK_DIM, tm)),
        _const_spec((HEAD_DIM, tm)), _const_spec((HEAD_DIM, tm)),
    ]
    return pl.pallas_call(
        _inproj_kernel, grid=grid, in_specs=in_specs, out_specs=out_specs, out_shape=out_shapes,
        compiler_params=_params(2), name="in_projection",
    )(x, g[None].astype(F32), w_rm, w_gate, w_t, cscale, cadd, tabs, bc(gqd), bc(gkd), bc(gqg), bc(gkg))


def _attn_kernel(*refs, combos, width, tq, tk, n_extra, finalize, bounded):
    post_ref, qt_ref, k_ref, vt_ref = refs[:4]
    extra = refs[4:4 + n_extra]
    o_ref = refs[4 + n_extra]
    m_sc, acc_sc, p_sc = refs[5 + n_extra:]
    n_kv = k_ref.shape[1] // tk
    n_c = len(combos)
    post = post_ref[0, 0]

    m_sc[...] = jnp.full(m_sc.shape, NEG_BIG, F32)
    acc_sc[...] = jnp.zeros(acc_sc.shape, F32)

    def scores(c, start):
        q_row0, k_lane0 = combos[c][0], combos[c][1]
        st = _dot(k_ref[0, pl.ds(start, tk), k_lane0:k_lane0 + width], qt_ref[0, q_row0:q_row0 + width, :])
        return st * post

    def values(c, start):
        v_row0 = combos[c][2]
        return vt_ref[0, v_row0:v_row0 + V_AUG, pl.ds(start, tk)]

    def qk_exp(c, start, slot):
        p_sc[slot, c] = jnp.exp2(scores(c, start).astype(BF16))

    def pv(c, start, slot):
        acc_sc[c] += _dot(values(c, start), p_sc[slot, c])

    def bounded_pair(i, carry):
        s0 = pl.multiple_of(2 * i * tk, tk)
        s1 = pl.multiple_of((2 * i + 1) * tk, tk)
        s2 = pl.multiple_of((2 * i + 2) * tk, tk)
        for c in range(n_c):
            qk_exp(c, s1, 1)
            pv(c, s0, 0)
        for c in range(n_c):
            qk_exp(c, s2, 0)
            pv(c, s1, 1)
        return carry

    def online_body(j, carry):
        start = pl.multiple_of(j * tk, tk)
        for c in range(n_c):
            st = scores(c, start)
            m_prev = m_sc[c]
            m_new = jnp.maximum(m_prev, jnp.max(st, axis=0, keepdims=True))
            p = jnp.exp2(st - m_new).astype(BF16)
            acc_sc[c] = jnp.exp2(m_prev - m_new) * acc_sc[c] + _dot(values(c, start), p)
            m_sc[c] = m_new
        return carry

    if bounded:
        for c in range(n_c):
            qk_exp(c, 0, 0)
        lax.fori_loop(0, n_kv // 2 - 1, bounded_pair, 0)
        for c in range(n_c):
            qk_exp(c, (n_kv - 1) * tk, 1)
            pv(c, (n_kv - 2) * tk, 0)
        for c in range(n_c):
            pv(c, (n_kv - 1) * tk, 1)
    else:
        lax.fori_loop(0, n_kv, online_body, 0)
    out_t = finalize([acc_sc[c] for c in range(len(combos))], extra)
    o_ref[0] = out_t.T.astype(BF16)


def _finalize_diff(accs, extra, *, lam_init):
    lam_ref, gsub_ref = extra
    lp = lam_ref[...]
    lam = (jnp.exp(jnp.sum(lp[0:1] * lp[1:2], axis=1, keepdims=True))
           - jnp.exp(jnp.sum(lp[2:3] * lp[3:4], axis=1, keepdims=True)) + lam_init)
    outs = []
    for h in range(N_HEADS):
        a1, a2 = accs[2 * h], accs[2 * h + 1]
        o = (a1[:HEAD_DIM] / a1[HEAD_DIM:HEAD_DIM + 1]
             - lam * (a2[:HEAD_DIM] / a2[HEAD_DIM:HEAD_DIM + 1]))
        ms = jnp.mean(o * o, axis=0, keepdims=True)
        outs.append(o * lax.rsqrt(ms + EPS) * gsub_ref[...] * (1.0 - lam_init))
    return jnp.concatenate(outs, axis=0)


def _finalize_gqa(accs, extra):
    return jnp.concatenate([a[:HEAD_DIM] / a[HEAD_DIM:HEAD_DIM + 1] for a in accs], axis=0)


def _gain_max(g):
    return jnp.maximum(jnp.max(jnp.abs(g)), 1e-30)


def _post_scale(g_q, g_k, dim, fp8_scale):
    return _gain_max(g_q) * _gain_max(g_k) * (dim ** -0.5) * LOG2E / (fp8_scale * fp8_scale)


def _score_bound(g_q, g_k, dim):
    return 1.02 * (dim ** 0.5) * LOG2E * jnp.max(jnp.abs(g_q)) * jnp.max(jnp.abs(g_k))


def _attention(qt, k, vt, extra, bound, post, *, name, combos, width, finalize, tq=512, tk=512):
    b, _, s = qt.shape
    n_c = len(combos)
    in_specs = [
        pl.BlockSpec(memory_space=pltpu.SMEM),
        pl.BlockSpec((1, qt.shape[1], tq), lambda bi, i: (bi, 0, i)),
        pl.BlockSpec((1, s, k.shape[2]), lambda bi, i: (bi, 0, 0)),
        pl.BlockSpec((1, vt.shape[1], s), lambda bi, i: (bi, 0, 0)),
    ] + [_const_spec(e.shape) for e in extra]

    def call(bounded):
        kern = functools.partial(_attn_kernel, combos=combos, width=width, tq=tq, tk=tk,
                                 n_extra=len(extra), finalize=finalize, bounded=bounded)
        return pl.pallas_call(
            kern, grid=(b, s // tq), in_specs=in_specs,
            out_specs=pl.BlockSpec((1, tq, 256), lambda bi, i: (bi, i, 0)),
            out_shape=jax.ShapeDtypeStruct((b, s, 256), BF16),
            scratch_shapes=[pltpu.VMEM((n_c, 1, tq), F32),
                            pltpu.VMEM((n_c, V_AUG, tq), F32),
                            pltpu.VMEM((2, n_c, tk, tq) if bounded else (1, 1, 16, LANES), BF16)],
            compiler_params=_params(2),
            name="attention_%s_%s" % (name, "bounded" if bounded else "online"),
        )(post.reshape(1, 1).astype(F32), qt, k, vt, *extra)

    return lax.cond(bound <= MAX_SCORE_BOUND, lambda: call(True), lambda: call(False))


def _diff_attention(qt, k, vt, lam_params, g_sub, lam_init, bound, post, tq=512):
    combos = tuple((128 * (2 * h + c), 128 * (2 * h + c), V_AUG * h)
                   for h in range(N_HEADS) for c in range(2))
    gsub_b = jnp.broadcast_to(g_sub.astype(F32)[:, None], (HEAD_DIM, tq))
    fin = functools.partial(_finalize_diff, lam_init=lam_init)
    return _attention(qt, k, vt, (lam_params.astype(F32), gsub_b), bound, post, name="diff", combos=combos,
                      width=4 * DIFF_QK_DIM, finalize=fin, tq=tq)


def _gqa_attention(qt, k, vt, bound, post, tq=512):
    combos = tuple((256 * qh, 256 * (qh // 2), V_AUG * (qh // 2)) for qh in range(N_HEADS))
    return _attention(qt, k, vt, (), bound, post, name="gqa", combos=combos, width=4 * HEAD_DIM,
                      finalize=_finalize_gqa, tq=tq)


def _log_sigmoid(x):
    return jnp.minimum(x, 0.0) - jnp.log(1.0 + jnp.exp(-jnp.abs(x)))


def _mlstm_kernel(qf_ref, kf_ref, vf_ref, ktf_ref, gf_ref, gtf_ref,
                  qb_ref, kb_ref, vb_ref, ktb_ref, gb_ref, gtb_ref,
                  brow_ref, bcol_ref, hf_ref, hb_ref, c_sc, m_sc):
    L = ML_CHUNK

    @pl.when(pl.program_id(1) == 0)
    def _():
        c_sc[...] = jnp.zeros(c_sc.shape, F32)
        m_sc[...] = jnp.zeros(m_sc.shape, F32)

    row = lax.broadcasted_iota(jnp.int32, (L, L), 0)
    col = lax.broadcasted_iota(jnp.int32, (L, L), 1)
    lower = col <= row
    upper = col >= row
    lower_f = lower.astype(F32)
    upper_f = upper.astype(F32)
    lane = lax.broadcasted_iota(jnp.int32, (1, LANES), 1)
    forget_lane = (lane % 8) >= 4
    grow = lax.broadcasted_iota(jnp.int32, (16, 1), 0)
    forget_row = (grow % 8) >= 4
    keep = (lax.broadcasted_iota(jnp.int32, (1, LANES), 1) < HEAD_DIM).astype(F32)

    fwd = (0, qf_ref, kf_ref, vf_ref, ktf_ref, gf_ref, gtf_ref, hf_ref, lower, lower_f, upper_f, L - 1)
    bwd = (1, qb_ref, kb_ref, vb_ref, ktb_ref, gb_ref, gtb_ref, hb_ref, upper, upper_f, lower_f, 0)
    work = []
    for j in range(ML_SUB):
        work += [fwd + (j * L,), bwd + ((ML_SUB - 1 - j) * L,)]
    for d, q_ref, k_ref, v_ref, kt_ref, g_ref, gt_ref, h_ref, mask, tri_c, tri_r, last, r0 in work:
        rows = slice(r0, r0 + L)
        g = g_ref[0, rows, :] + brow_ref[...]
        gp = jnp.where(forget_lane, _log_sigmoid(g), g)
        gt = gt_ref[0, :, rows] + bcol_ref[...]
        gtp = jnp.where(forget_row, _log_sigmoid(gt), gt)
        bcol_all = _dot_f32(tri_c, gp)
        brow_all = _dot_f32(gtp, tri_r)
        for h in range(N_HEADS):
            ii, fi = d * 8 + h, d * 8 + 4 + h
            idx = d * N_HEADS + h
            b_col = bcol_all[:, fi:fi + 1]
            b_row = brow_all[fi:fi + 1, :]
            c_row = gtp[ii:ii + 1, :] - b_row
            g_tot = b_row[:, last:last + 1]
            m_prev = m_sc[idx][0:1, 0:1]

            qh = q_ref[0, rows, h * LANES:(h + 1) * LANES]
            kh = k_ref[0, rows, h * LANES:(h + 1) * LANES]
            vh = v_ref[0, rows, h * LANES:(h + 1) * LANES]
            kth = kt_ref[0, h * LANES:(h + 1) * LANES, rows]

            cmask = jnp.where(mask, c_row, -jnp.inf)
            mu = jnp.maximum(m_prev, jnp.max(cmask, axis=1, keepdims=True))
            wts = (jnp.exp(cmask - mu) * _dot_nt(qh, kh)).astype(BF16)
            s_inter = jnp.exp(m_prev - mu)
            c_prev = c_sc[idx]
            tot = _dot(wts, vh) + s_inter * _dot(qh, c_prev.astype(BF16))
            den = tot[:, HEAD_DIM:HEAD_DIM + 1]
            hout = tot / jnp.maximum(jnp.abs(den), jnp.exp(-(b_col + mu)))
            h_ref[0, rows, h * LANES:(h + 1) * LANES] = (hout * keep).astype(BF16)

            c_max = jnp.max(c_row, axis=1, keepdims=True)
            m_loc = g_tot + c_max
            w_row = jnp.exp(c_row - c_max)
            c_loc = _dot((kth.astype(F32) * w_row).astype(BF16), vh)
            m_new = jnp.maximum(g_tot + m_prev, m_loc)
            c_sc[idx] = jnp.exp(g_tot + m_prev - m_new) * c_prev + jnp.exp(m_loc - m_new) * c_loc
            m_sc[idx] = jnp.broadcast_to(m_new, (SUBLANES, LANES))


def _mlstm(rm, mkt, gates, gates_t, b_i, b_f):
    b, s, _ = rm.shape
    L = ML_CHUNK * ML_SUB
    nc = s // L
    bias = jnp.stack([b_i[0], b_f[0], b_i[1], b_f[1]]).reshape(16).astype(F32)
    bias_row = jnp.concatenate([bias, jnp.zeros((LANES - 16,), F32)])[None]
    bias_col = jnp.broadcast_to(bias[:, None], (16, ML_CHUNK))
    fwd = lambda cblk: (lambda bi, i: (bi, i, cblk))
    bwd = lambda cblk: (lambda bi, i: (bi, nc - 1 - i, cblk))
    in_specs = []
    for mk in (fwd, bwd):
        in_specs += [pl.BlockSpec((1, L, 512), mk(RM_Q // 512)),
                     pl.BlockSpec((1, L, 512), mk(RM_K // 512)),
                     pl.BlockSpec((1, L, 512), mk(RM_V // 512)),
                     pl.BlockSpec((1, 512, L), (lambda bi, i: (bi, 0, i)) if mk is fwd
                                  else (lambda bi, i: (bi, 0, nc - 1 - i))),
                     pl.BlockSpec((1, L, LANES), mk(0)),
                     pl.BlockSpec((1, 16, L), (lambda bi, i: (bi, 0, i)) if mk is fwd
                                  else (lambda bi, i: (bi, 0, nc - 1 - i)))]
    in_specs += [_const_spec((1, LANES)), _const_spec((16, ML_CHUNK))]
    out_specs = (pl.BlockSpec((1, L, 512), lambda bi, i: (bi, i, 0)),
                 pl.BlockSpec((1, L, 512), lambda bi, i: (bi, nc - 1 - i, 0)))
    out_shape = (jax.ShapeDtypeStruct((b, s, 512), BF16), jax.ShapeDtypeStruct((b, s, 512), BF16))
    return pl.pallas_call(
        _mlstm_kernel, grid=(b, nc), in_specs=in_specs, out_specs=out_specs, out_shape=out_shape,
        scratch_shapes=[pltpu.VMEM((2 * N_HEADS, LANES, LANES), F32),
                        pltpu.VMEM((2 * N_HEADS, SUBLANES, LANES), F32)],
        compiler_params=_params(2), name="mlstm",
    )(rm, rm, rm, mkt, gates, gates_t, rm, rm, rm, mkt, gates, gates_t, bias_row, bias_col)


def _outproj_cross_kernel(hf_ref, hb_ref, o_ref, pm_ref, pp_ref, pn_ref, yc_ref, yd_ref, x_ref,
                          gml_ref, wpool_ref, pscale_ref, wa_ref, wb_ref, wc_ref, wd_ref,
                          gx_ref, wq_ref, gq_ref, k_ref, v_ref, wo_ref, out_ref, *, tm, seq):
    i = pl.program_id(1)
    n_t = pl.num_programs(1)

    hs = hf_ref[0].astype(F32) + hb_ref[0].astype(F32)
    og = o_ref[0].astype(F32)
    parts = []
    for h in range(N_HEADS):
        sl = slice(h * LANES, (h + 1) * LANES)
        hh = hs[:, sl]
        ms = jnp.sum(hh * hh, axis=-1, keepdims=True) * (1.0 / HEAD_DIM)
        gate = 1.0 / (1.0 + jnp.exp(-og[:, sl]))
        parts.append(hh * lax.rsqrt(ms + EPS) * gml_ref[:, sl] * gate)
    ya = jnp.concatenate(parts, axis=1).astype(BF16)

    prev = jnp.where(i > 0, pp_ref[0].astype(F32), 0.0)
    nxt = jnp.where(i < n_t - 1, pn_ref[0].astype(F32), 0.0)
    u = jnp.concatenate([prev, pm_ref[0].astype(F32), nxt], axis=0)
    n = tm + 2 * SUBLANES
    sh = lambda v, k: pltpu.roll(v, k % n, 0)
    w2 = u + sh(u, 1)
    w4 = sh(w2, 1) + sh(w2, -1)
    w8 = sh(w4, 2) + sh(w4, -2)
    w16 = sh(w8, 4) + sh(w8, -4)
    core = slice(SUBLANES, SUBLANES + tm)
    lane = lax.broadcasted_iota(jnp.int32, (1, GROUP_WIDTH), 1)
    t = (i * tm + lax.broadcasted_iota(jnp.int32, (tm, 1), 0)).astype(F32)
    total = jnp.where(lane < 64, w2[core], jnp.where(lane < 128, w4[core],
                      jnp.where(lane < 192, w8[core], w16[core])))
    half = jnp.where(lane < 64, 1.0, jnp.where(lane < 128, 2.0, jnp.where(lane < 192, 4.0, 8.0)))
    lo = jnp.maximum(t - half, 0.0)
    hi = jnp.minimum(t + half - 1.0, seq - 1.0)
    pooled = total / (hi - lo + 1.0) - u[core]
    yb = (_dot(pooled.astype(BF16), wpool_ref[...]) * pscale_ref[...]).astype(BF16)

    x1 = (x_ref[0] + _dot(ya, wa_ref[...]) + _dot(yb, wb_ref[...])
          + _dot(yc_ref[0], wc_ref[...]) + _dot(yd_ref[0], wd_ref[...]))
    out_ref[0] = x1 + _cross_delta(x1, gx_ref, wq_ref, gq_ref, k_ref, v_ref, wo_ref)


def _out_projection_cross(hf, hb, rm, yc, yd, x, ml_norm, pool_w, pool_scale, w_out,
                          g_x, w_q, g_q, k, v, w_o, tm=512):
    b, s, d = x.shape
    wa = w_out[0:256].reshape(N_HEADS, HEAD_DIM, d)
    wa = jnp.concatenate([wa, jnp.zeros_like(wa)], axis=1).reshape(N_HEADS * LANES, d).astype(BF16)
    wb, wc, wd = (w_out[256 * j:256 * (j + 1)].astype(BF16) for j in (1, 2, 3))
    gml = ml_norm.reshape(N_HEADS, HEAD_DIM)
    gml = jnp.concatenate([gml, jnp.zeros_like(gml)], axis=1).reshape(1, N_HEADS * LANES).astype(F32)
    wpool = jax.scipy.linalg.block_diag(*[pool_w[g] for g in range(4)]).astype(BF16)
    nb8 = tm // SUBLANES
    last8 = s // SUBLANES - 1
    row = lambda w, cb=0: pl.BlockSpec((1, tm, w), lambda bi, i: (bi, i, cb))
    in_specs = [
        row(512), row(512), row(512, RM_O // 512),
        row(256, RM_POOL // 256),
        pl.BlockSpec((1, SUBLANES, 256), lambda bi, i: (bi, jnp.maximum(i * nb8 - 1, 0), RM_POOL // 256)),
        pl.BlockSpec((1, SUBLANES, 256), lambda bi, i: (bi, jnp.minimum((i + 1) * nb8, last8), RM_POOL // 256)),
        row(256), row(256), row(d),
        _const_spec((1, 512)), _const_spec((256, 256)), _const_spec((1, 256)),
        _const_spec((512, d)), _const_spec((256, d)), _const_spec((256, d)), _const_spec((256, d)),
        _const_spec((1, d)), _const_spec((d, d)), _const_spec((1, CROSS_HEAD_DIM)),
        pl.BlockSpec((1, k.shape[1], d), lambda bi, i: (bi, 0, 0)),
        pl.BlockSpec((1, k.shape[1], d), lambda bi, i: (bi, 0, 0)),
        _const_spec((d, d)),
    ]
    kern = functools.partial(_outproj_cross_kernel, tm=tm, seq=s)
    return pl.pallas_call(
        kern, grid=(b, s // tm), in_specs=in_specs, out_specs=row(d),
        out_shape=jax.ShapeDtypeStruct((b, s, d), F32),
        compiler_params=_params(2), name="out_projection_cross",
    )(hf, hb, rm, rm, rm, rm, yc, yd, x, gml, wpool, pool_scale[None].astype(F32), wa, wb, wc, wd,
      g_x[None].astype(F32), w_q.astype(BF16), g_q[None].astype(F32), k, v, w_o.astype(BF16))


def _memkv_kernel(mem_ref, g_ref, w_ref, gk_ref, k_ref, v_ref):
    mn = _rms(mem_ref[0], g_ref[...]).astype(BF16)
    kv = _dot(mn, w_ref[...])
    ks = []
    for h in range(CROSS_HEADS):
        ks.append(_rms(kv[:, h * CROSS_HEAD_DIM:(h + 1) * CROSS_HEAD_DIM], gk_ref[...]))
    k_ref[0] = jnp.concatenate(ks, axis=1).astype(BF16)
    v_ref[0] = kv[:, D_MODEL:].astype(BF16)


def _mem_kv(mem, g_mem, w_kv, g_k):
    b, m, d = mem.shape
    blk = pl.BlockSpec((1, m, d), lambda bi: (bi, 0, 0))
    return pl.pallas_call(
        _memkv_kernel, grid=(b,),
        in_specs=[blk, _const_spec((1, d)), _const_spec((d, 2 * d)), _const_spec((1, CROSS_HEAD_DIM))],
        out_specs=(blk, blk),
        out_shape=(jax.ShapeDtypeStruct((b, m, d), BF16), jax.ShapeDtypeStruct((b, m, d), BF16)),
        compiler_params=_params(1), name="mem_kv",
    )(mem, g_mem[None].astype(F32), w_kv.astype(BF16), g_k[None].astype(F32))


def _cross_delta(x, g_ref, wq_ref, gq_ref, k_ref, v_ref, wo_ref):
    xn = _rms(x, g_ref[...]).astype(BF16)
    q = _dot(xn, wq_ref[...])
    outs = []
    for h in range(CROSS_HEADS):
        sl = slice(h * CROSS_HEAD_DIM, (h + 1) * CROSS_HEAD_DIM)
        qh = (_rms(q[:, sl], gq_ref[...]) * (CROSS_HEAD_DIM ** -0.5)).astype(BF16)
        s = _dot_nt(qh, k_ref[0, :, sl])
        p = jnp.exp(s - jnp.max(s, axis=-1, keepdims=True))
        l = jnp.sum(p, axis=-1, keepdims=True)
        outs.append((_dot(p.astype(BF16), v_ref[0, :, sl]) / l).astype(BF16))
    return _dot(jnp.concatenate(outs, axis=1), wo_ref[...])


FF_CHUNK = 256


def _ffn_kernel(xm_ref, xp_ref, xn_ref, g_ref, win_ref, cw_ref, cb_ref, wout_ref, out_ref, *, tm):
    i = pl.program_id(1)
    n_t = pl.num_programs(1)
    xm = xm_ref[0]
    xe = jnp.concatenate([xp_ref[0], xm, xn_ref[0]], axis=0)
    hn = _rms(xe, g_ref[...]).astype(BF16)
    n = tm + 2 * SUBLANES
    core = slice(SUBLANES, SUBLANES + tm)
    hc = hn[core]
    r = lax.broadcasted_iota(jnp.int32, (n, 1), 0)
    live = jnp.logical_and(jnp.logical_or(i > 0, r >= SUBLANES),
                           jnp.logical_or(i < n_t - 1, r < SUBLANES + tm)).astype(F32)
    acc = xm
    for c in range(D_FF // FF_CHUNK):
        gs = slice(c * FF_CHUNK, (c + 1) * FF_CHUNK)
        us = slice(D_FF + c * FF_CHUNK, D_FF + (c + 1) * FF_CHUNK)
        gate = _dot(hn, win_ref[:, gs]) * live
        cw = cw_ref[:, gs]
        gconv = (pltpu.roll(gate, 1, 0)[core] * cw[0:1] + gate[core] * cw[1:2]
                 + pltpu.roll(gate, n - 1, 0)[core] * cw[2:3] + cb_ref[:, gs])
        up = _dot(hc, win_ref[:, us])
        act = (gconv / (1.0 + jnp.exp(-gconv)) * up).astype(BF16)
        acc = acc + _dot(act, wout_ref[gs, :])
    out_ref[0] = acc


def _conv_ffn(x, g, w_in, conv_w, conv_b, w_out, tm=512):
    b, s, d = x.shape
    nb8 = tm // SUBLANES
    last8 = s // SUBLANES - 1
    row = pl.BlockSpec((1, tm, d), lambda bi, i: (bi, i, 0))
    in_specs = [
        row,
        pl.BlockSpec((1, SUBLANES, d), lambda bi, i: (bi, jnp.maximum(i * nb8 - 1, 0), 0)),
        pl.BlockSpec((1, SUBLANES, d), lambda bi, i: (bi, jnp.minimum((i + 1) * nb8, last8), 0)),
        _const_spec((1, d)), _const_spec((d, 2 * D_FF)), _const_spec((3, D_FF)), _const_spec((1, D_FF)),
        _const_spec((D_FF, d)),
    ]
    return pl.pallas_call(
        functools.partial(_ffn_kernel, tm=tm), grid=(b, s // tm), in_specs=in_specs, out_specs=row,
        out_shape=jax.ShapeDtypeStruct((b, s, d), F32),
        compiler_params=_params(2), name="conv_ffn",
    )(x, x, x, g[None].astype(F32), w_in.astype(BF16), conv_w.astype(F32), conv_b[None].astype(F32),
      w_out.astype(BF16))


def _rope_tables(s):
    pos = jnp.arange(s)
    row = pos // GRID_W
    col = pos - row * GRID_W
    inv = ROPE_THETA ** (-jnp.arange(0, DIFF_QK_DIM, 2, dtype=F32) / DIFF_QK_DIM)
    out = []
    for p in (pos, row, col):
        ang = inv[:, None] * p.astype(F32)[None, :]
        out += [jnp.cos(ang), jnp.sin(ang)]
    return jnp.stack(out)


def kernel(x, mem, norm_mix, w_in, ml_bias_i, ml_bias_f, ml_norm, pool_w, pool_scale, diff_qnorm, diff_knorm, diff_lambda, diff_subnorm, gqa_qnorm, gqa_knorm, w_out, norm_cross, norm_mem, w_cq, w_ckv, cross_qnorm, cross_knorm, w_co, norm_ffn, w_ffn_in, ffn_conv, ffn_conv_b, w_ffn_out):
    depth = w_in.shape[0]
    tabs = _rope_tables(x.shape[1])
    for l in range(depth):
        lam_init = 0.8 - 0.6 * math.exp(-0.3 * l)
        w_rm, w_gate, w_t = _inproj_weights(w_in[l])
        (rm, gates, gates_t, qtd, kd, vtd, qtg, kg, vtg, mkt) = _in_projection(
            x, norm_mix[l], w_rm, w_gate, w_t, tabs, diff_qnorm[l] / _gain_max(diff_qnorm[l]),
            diff_knorm[l] / _gain_max(diff_knorm[l]), gqa_qnorm[l] / _gain_max(gqa_qnorm[l]),
            gqa_knorm[l] / _gain_max(gqa_knorm[l]))
        hf, hb = _mlstm(rm, mkt, gates, gates_t, ml_bias_i[l], ml_bias_f[l])
        yc = _diff_attention(qtd, kd, vtd, diff_lambda[l], diff_subnorm[l], lam_init,
                             _score_bound(diff_qnorm[l], diff_knorm[l], DIFF_QK_DIM),
                             _post_scale(diff_qnorm[l], diff_knorm[l], DIFF_QK_DIM, FP8_SCALE_DIFF))
        yd = _gqa_attention(qtg, kg, vtg, _score_bound(gqa_qnorm[l], gqa_knorm[l], HEAD_DIM),
                            _post_scale(gqa_qnorm[l], gqa_knorm[l], HEAD_DIM, FP8_SCALE_GQA))
        ck, cv = _mem_kv(mem, norm_mem[l], w_ckv[l], cross_knorm[l])
        x = _out_projection_cross(hf, hb, rm, yc, yd, x, ml_norm[l], pool_w[l], pool_scale[l], w_out[l],
                                  norm_cross[l], w_cq[l], cross_qnorm[l], ck, cv, w_co[l])
        x = _conv_ffn(x, norm_ffn[l], w_ffn_in[l], ffn_conv[l], ffn_conv_b[l], w_ffn_out[l])
    return x
```
